```python
import math
import jax
import jax.numpy as jnp
from jax import lax
import numpy as np

D_MODEL = 1024
BATCH = 4
SEQ = 4096
DEPTH = 1
DEC_BATCH = 128
DEC_SEQ = 1
PAST_LEN = 8192
PAGE_SIZE = 128

N_HEADS_A = 4
HEAD_DIM_A = 64
QK_WIDTH_A = N_HEADS_A * 2 * HEAD_DIM_A
V_WIDTH_A = N_HEADS_A * 2 * HEAD_DIM_A
N_HEADS_B = 8
HEAD_DIM_B = 64
WIDTH_B = N_HEADS_B * HEAD_DIM_B
N_BRANCHES = 2
IN_WIDTH = 2 * QK_WIDTH_A + V_WIDTH_A + 3 * WIDTH_B + N_BRANCHES * D_MODEL
D_FF = 2816
ROPE_THETA = 10000.0
RMS_EPS = 1e-6
Q_BLOCK = 128

kernel_name = "diff_stickbreaking_gated_macaron_decode_step"


def lambda_init_fn(layer_idx):
    return 0.8 - 0.6 * math.exp(-0.3 * layer_idx)


def rms_norm(x, gain):
    xf = x.astype(jnp.float32)
    y = xf * lax.rsqrt(jnp.mean(xf * xf, axis=-1, keepdims=True) + RMS_EPS)
    return (y * gain.astype(jnp.float32)).astype(x.dtype)


def swiglu(x, w_up, w_down):
    g, u = jnp.split(x @ w_up, 2, axis=-1)
    return (jax.nn.silu(g) * u) @ w_down


def rope(x, pos):
    d = x.shape[-1]
    half = d // 2
    inv = ROPE_THETA ** (-jnp.arange(half, dtype=jnp.float32) / half)
    ang = pos.astype(jnp.float32)[:, None] * inv[None, :]
    ang = jnp.concatenate([ang, ang], axis=-1)
    shape = (ang.shape[0],) + (1,) * (x.ndim - 3) + (d,)
    cos = jnp.cos(ang).reshape(shape)
    sin = jnp.sin(ang).reshape(shape)
    xf = x.astype(jnp.float32)
    rot = jnp.concatenate([-xf[..., half:], xf[..., :half]], axis=-1)
    return (xf * cos + rot * sin).astype(x.dtype)


def project(h, pos, w_in, g_q_a, g_k_a):
    b, t, _ = h.shape
    proj = h @ w_in
    o = [int(s) for s in np.cumsum([QK_WIDTH_A, QK_WIDTH_A, V_WIDTH_A, WIDTH_B, WIDTH_B, WIDTH_B])]
    qa = proj[..., :o[0]].reshape(b, t, N_HEADS_A, 2, HEAD_DIM_A)
    ka = proj[..., o[0]:o[1]].reshape(b, t, N_HEADS_A, 2, HEAD_DIM_A)
    va = proj[..., o[1]:o[2]].reshape(b, t, N_HEADS_A, 2 * HEAD_DIM_A)
    qb = proj[..., o[2]:o[3]].reshape(b, t, N_HEADS_B, HEAD_DIM_B)
    kb = proj[..., o[3]:o[4]].reshape(b, t, N_HEADS_B, HEAD_DIM_B)
    vb = proj[..., o[4]:o[5]].reshape(b, t, N_HEADS_B, HEAD_DIM_B)
    gates = proj[..., o[5]:].reshape(b, t, N_BRANCHES, D_MODEL)
    qa = rope(rms_norm(qa, g_q_a), pos)
    ka = rope(rms_norm(ka, g_k_a), pos)
    return qa, ka, va, qb, kb, vb, gates


def diff_attention(q, segments, q_pos, lam):
    scale = HEAD_DIM_A ** -0.5
    s = jnp.concatenate([
        jnp.where(kp[None, :] <= q_pos[:, None],
                  jnp.einsum('bqhmd,bkhmd->bhmqk', q, k, preferred_element_type=jnp.float32) * scale,
                  -jnp.inf)
        for k, _, kp in segments], axis=-1)
    p = jax.nn.softmax(s, axis=-1)
    w = p[:, :, 0] - lam * p[:, :, 1]
    out = None
    start = 0
    for k, v, _ in segments:
        tk = k.shape[1]
        part = jnp.einsum('bhqk,bkhe->bqhe', w[..., start:start + tk].astype(v.dtype), v)
        out = part if out is None else out + part
        start += tk
    return out


def stick_breaking_attention(q, segments, q_pos):
    scale = HEAD_DIM_B ** -0.5
    z = jnp.concatenate([
        jnp.einsum('bqhd,bkhd->bhqk', q, k, preferred_element_type=jnp.float32) * scale
        for k, _, _ in segments], axis=-1)
    k_pos = jnp.concatenate([kp for _, _, kp in segments], axis=0)
    mask = k_pos[None, :] < q_pos[:, None]
    log_beta = jax.nn.log_sigmoid(z)
    log_1mb = jnp.where(mask, jax.nn.log_sigmoid(-z), 0.0)
    rev = lax.cumsum(log_1mb, axis=log_1mb.ndim - 1, reverse=True)
    suffix = jnp.concatenate([rev[..., 1:], jnp.zeros_like(rev[..., :1])], axis=-1)
    a = jnp.where(mask, jnp.exp(log_beta + suffix), 0.0)
    out = None
    start = 0
    for k, v, _ in segments:
        tk = k.shape[1]
        part = jnp.einsum('bhqk,bkhd->bqhd', a[..., start:start + tk].astype(v.dtype), v)
        out = part if out is None else out + part
        start += tk
    return out


def prompt_attend(qa, ka, va, qb, kb, vb, pos, lam):
    b, t = qa.shape[:2]
    nb = t // Q_BLOCK

    def to_blocks(q):
        return jnp.moveaxis(q.reshape((b, nb, Q_BLOCK) + q.shape[2:]), 1, 0)

    def from_blocks(o):
        return jnp.moveaxis(o, 0, 1).reshape((b, t) + o.shape[3:])

    pos_blocks = pos.reshape(nb, Q_BLOCK)
    oa = lax.map(lambda a: diff_attention(a[0], ((ka, va, pos),), a[1], lam), (to_blocks(qa), pos_blocks))
    ob = lax.map(lambda a: stick_breaking_attention(a[0], ((kb, vb, pos),), a[1]), (to_blocks(qb), pos_blocks))
    return from_blocks(oa), from_blocks(ob)


def gather_pages(cache_layer, page_table):
    g = cache_layer[page_table]
    return g.reshape((page_table.shape[0], page_table.shape[1] * cache_layer.shape[1]) + cache_layer.shape[2:])


def make_sample_attend(past_ka, past_va, past_kb, past_vb):
    past_pos = jnp.arange(past_ka.shape[1], dtype=jnp.int32)

    def attend(qa, ka, va, qb, kb, vb, pos, lam):
        oa = diff_attention(qa, ((past_ka, past_va, past_pos), (ka, va, pos)), pos, lam)
        ob = stick_breaking_attention(qb, ((past_kb, past_vb, past_pos), (kb, vb, pos)), pos)
        return oa, ob
    return attend


def trunk_layer(x, pos, attend, lp, lambda_init):
    (g_ffn1, w_ffn1_up, w_ffn1_down, g_mix, w_in, g_q_a, g_k_a, lq1, lk1, lq2, lk2,
     g_subln_a, w_o_a, w_o_b, w_out, g_ffn2, w_ffn2_up, w_ffn2_down) = lp
    b, t = x.shape[:2]
    x = x + 0.5 * swiglu(rms_norm(x, g_ffn1), w_ffn1_up, w_ffn1_down)
    h = rms_norm(x, g_mix)
    qa, ka, va, qb, kb, vb, gates = project(h, pos, w_in, g_q_a, g_k_a)
    lam = (jnp.exp(jnp.sum(lq1.astype(jnp.float32) * lk1.astype(jnp.float32)))
           - jnp.exp(jnp.sum(lq2.astype(jnp.float32) * lk2.astype(jnp.float32))) + lambda_init)
    oa, ob = attend(qa, ka, va, qb, kb, vb, pos, lam)
    oa = (rms_norm(oa, g_subln_a) * (1.0 - lambda_init)).reshape(b, t, V_WIDTH_A)
    ob = ob.reshape(b, t, WIDTH_B)
    gate = jax.nn.sigmoid(gates)
    merged = gate[:, :, 0] * (oa @ w_o_a) + gate[:, :, 1] * (ob @ w_o_b)
    x = x + merged @ w_out
    x = x + 0.5 * swiglu(rms_norm(x, g_ffn2), w_ffn2_up, w_ffn2_down)
    return x, (ka, va, kb, vb)


def setup_inputs(seed: int = 0) -> dict:
    key = jax.random.key(seed)
    ks = jax.random.split(key, 32)
    f32 = jnp.float32
    n_pages = PAST_LEN // PAGE_SIZE
    n_used = DEC_BATCH * n_pages
    n_phys = n_used + max(1, n_used // 4)

    def normal(k, shape, scale=1.0):
        return jax.random.normal(k, shape, f32) * scale

    def gain(k, n):
        return 1.0 + 0.02 * jax.random.normal(k, (DEPTH, n), f32)

    page_table = jax.random.permutation(ks[0], n_phys)[:n_used].reshape(DEC_BATCH, n_pages).astype(jnp.int32)
    return {
        "x_prompt": normal(ks[1], (BATCH, SEQ, D_MODEL)),
        "x_sample": normal(ks[2], (DEC_BATCH, DEC_SEQ, D_MODEL)),
        "cache_a_k": normal(ks[3], (DEPTH, n_phys, PAGE_SIZE, N_HEADS_A, 2, HEAD_DIM_A)),
        "cache_a_v": normal(ks[4], (DEPTH, n_phys, PAGE_SIZE, N_HEADS_A, 2 * HEAD_DIM_A)),
        "cache_b_k": normal(ks[5], (DEPTH, n_phys, PAGE_SIZE, N_HEADS_B, HEAD_DIM_B)),
        "cache_b_v": normal(ks[6], (DEPTH, n_phys, PAGE_SIZE, N_HEADS_B, HEAD_DIM_B)),
        "page_table": page_table,
        "g_ffn1": gain(ks[7], D_MODEL),
        "w_ffn1_up": normal(ks[8], (DEPTH, D_MODEL, 2 * D_FF), D_MODEL ** -0.5),
        "w_ffn1_down": normal(ks[9], (DEPTH, D_FF, D_MODEL), D_FF ** -0.5),
        "g_mix": gain(ks[10], D_MODEL),
        "w_in": normal(ks[11], (DEPTH, D_MODEL, IN_WIDTH), D_MODEL ** -0.5),
        "g_q_a": gain(ks[12], HEAD_DIM_A),
        "g_k_a": gain(ks[13], HEAD_DIM_A),
        "lambda_q1": normal(ks[14], (DEPTH, HEAD_DIM_A), 0.1),
        "lambda_k1": normal(ks[15], (DEPTH, HEAD_DIM_A), 0.1),
        "lambda_q2": normal(ks[16], (DEPTH, HEAD_DIM_A), 0.1),
        "lambda_k2": normal(ks[17], (DEPTH, HEAD_DIM_A), 0.1),
        "g_subln_a": gain(ks[18], 2 * HEAD_DIM_A),
        "w_o_a": normal(ks[19], (DEPTH, V_WIDTH_A, D_MODEL), V_WIDTH_A ** -0.5),
        "w_o_b": normal(ks[20], (DEPTH, WIDTH_B, D_MODEL), WIDTH_B ** -0.5),
        "w_out": normal(ks[21], (DEPTH, D_MODEL, D_MODEL), D_MODEL ** -0.5),
        "g_ffn2": gain(ks[22], D_MODEL),
        "w_ffn2_up": normal(ks[23], (DEPTH, D_MODEL, 2 * D_FF), D_MODEL ** -0.5),
        "w_ffn2_down": normal(ks[24], (DEPTH, D_FF, D_MODEL), D_FF ** -0.5),
    }


def reference(x_prompt, x_sample, cache_a_k, cache_a_v, cache_b_k, cache_b_v, page_table,
              g_ffn1, w_ffn1_up, w_ffn1_down, g_mix, w_in, g_q_a, g_k_a,
              lambda_q1, lambda_k1, lambda_q2, lambda_k2, g_subln_a, w_o_a, w_o_b, w_out,
              g_ffn2, w_ffn2_up, w_ffn2_down):
    past_len = page_table.shape[1] * cache_a_k.shape[2]
    pos_prompt = jnp.arange(x_prompt.shape[1], dtype=jnp.int32)
    pos_sample = past_len + jnp.arange(x_sample.shape[1], dtype=jnp.int32)
    xp, xs = x_prompt, x_sample
    new_p = ([], [], [], [])
    new_s = ([], [], [], [])
    for l in range(DEPTH):
        lp = (g_ffn1[l], w_ffn1_up[l], w_ffn1_down[l], g_mix[l], w_in[l], g_q_a[l], g_k_a[l],
              lambda_q1[l], lambda_k1[l], lambda_q2[l], lambda_k2[l], g_subln_a[l],
              w_o_a[l], w_o_b[l], w_out[l], g_ffn2[l], w_ffn2_up[l], w_ffn2_down[l])
        lam_init = lambda_init_fn(l)
        xp, rows_p = trunk_layer(xp, pos_prompt, prompt_attend, lp, lam_init)
        sample_attend = make_sample_attend(gather_pages(cache_a_k[l], page_table),
                                           gather_pages(cache_a_v[l], page_table),
                                           gather_pages(cache_b_k[l], page_table),
                                           gather_pages(cache_b_v[l], page_table))
        xs, rows_s = trunk_layer(xs, pos_sample, sample_attend, lp, lam_init)
        for i in range(4):
            new_p[i].append(rows_p[i])
            new_s[i].append(rows_s[i])
    new_a_k_prompt = jnp.stack(new_p[0])
    new_a_v_prompt = jnp.stack(new_p[1])
    new_b_k_prompt = jnp.stack(new_p[2])
    new_b_v_prompt = jnp.stack(new_p[3])
    new_a_k_sample = jnp.stack(new_s[0])
    new_a_v_sample = jnp.stack(new_s[1])
    new_b_k_sample = jnp.stack(new_s[2])
    new_b_v_sample = jnp.stack(new_s[3])
    return (xp, xs, new_a_k_prompt, new_a_v_prompt, new_b_k_prompt, new_b_v_prompt,
            new_a_k_sample, new_a_v_sample, new_b_k_sample, new_b_v_sample)
```

```python
import functools
import math

import jax
import jax.numpy as jnp
from jax import lax
from jax.experimental import pallas as pl
from jax.experimental.pallas import tpu as pltpu

F32 = jnp.float32
BF16 = jnp.bfloat16

RMS_EPS = 1e-6
ROPE_THETA = 10000.0
HEAD_DIM = 64
LANES = 128
VMEM_LIMIT_BYTES = 56 * 1024 * 1024
FF_CHUNK = 256
ATTN_BLOCK = 256
PAGES_PER_STEP = 8


def _lambda_init(layer_idx):
    return 0.8 - 0.6 * math.exp(-0.3 * layer_idx)


def _const_spec(shape):
    nd = len(shape)
    return pl.BlockSpec(shape, lambda *_: (0,) * nd, pipeline_mode=pl.Buffered(1))


def _params(sem):
    return pltpu.CompilerParams(dimension_semantics=sem, vmem_limit_bytes=VMEM_LIMIT_BYTES)


def _rms(x, g):
    return x * lax.rsqrt(jnp.mean(x * x, axis=-1, keepdims=True) + RMS_EPS) * g


def _dot(a, b):
    return jnp.dot(a, b, preferred_element_type=F32)


def _dot_nt(a, b):
    return lax.dot_general(a, b, (((1,), (1,)), ((), ())), preferred_element_type=F32)


def _split_dot(x, w):
    hi = x.astype(BF16)
    lo = (x - hi.astype(F32)).astype(BF16)
    return _dot(hi, w) + _dot(lo, w)


def _swiglu_into(xb, wup_ref, h_scr, dff):
    for c in range(dff // FF_CHUNK):
        lo = c * FF_CHUNK
        g = _dot(xb, wup_ref[:, lo:lo + FF_CHUNK])
        u = _dot(xb, wup_ref[:, dff + lo:dff + lo + FF_CHUNK])
        h_scr[:, lo:lo + FF_CHUNK] = (g * (1.0 / (1.0 + jnp.exp(-g))) * u).astype(BF16)


def _ffn_body(x_ref, g_ref, wup_ref, wdn_ref, o_ref, h_scr, *, dff):
    x = x_ref[...]
    _swiglu_into(_rms(x, g_ref[...]).astype(BF16), wup_ref, h_scr, dff)
    o_ref[...] = x + 0.5 * _dot(h_scr[...], wdn_ref[...])


def _ffn(x, g, wup, wdn, tm):
    n, d = x.shape
    dff = wdn.shape[0]
    row = pl.BlockSpec((tm, d), lambda i: (i, 0))
    return pl.pallas_call(
        functools.partial(_ffn_body, dff=dff),
        grid=(n // tm,),
        in_specs=[row, _const_spec((1, d)), _const_spec(wup.shape), _const_spec(wdn.shape)],
        out_specs=row,
        out_shape=jax.ShapeDtypeStruct((n, d), F32),
        scratch_shapes=[pltpu.VMEM((tm, dff), BF16)],
        compiler_params=_params(("parallel",)),
        name="ffn",
    )(x, g, wup, wdn)


def _swap_halves(x):
    lane = lax.broadcasted_iota(jnp.int32, (x.shape[0], LANES), 1)
    first = (lane % HEAD_DIM) < (HEAD_DIM // 2)
    out = []
    for c in range(x.shape[1] // LANES):
        xs = x[:, c * LANES:(c + 1) * LANES]
        up = pltpu.roll(xs, LANES - HEAD_DIM // 2, axis=1)
        down = pltpu.roll(xs, HEAD_DIM // 2, axis=1)
        out.append(jnp.where(first, up, down))
    return jnp.concatenate(out, axis=1)


def _qk_norm_rope(x, g, cos, sin_signed, group_ones):
    ms = _split_dot(x * x, group_ones) * (1.0 / HEAD_DIM)
    y = x * lax.rsqrt(ms + RMS_EPS) * g
    return y * cos + _swap_halves(y) * sin_signed


def _proj_body(x_ref, g_ref, w_ref, gq_ref, gk_ref, cos_ref, sin_ref, ones_ref,
               ka_o, va_o, kb_o, vb_o, qa_b, ka_b, va_b, qb_b, kb_b, vb_b, *, width, scale):
    h = _rms(x_ref[...], g_ref[...]).astype(BF16)

    def seg(i):
        return _dot(h, w_ref[:, i * width:(i + 1) * width])

    cos, sin = cos_ref[...], sin_ref[...]
    qa =_qk_norm_rope(seg(0), gq_ref[...], cos, sin, ones_ref[...])
    qa_b[...] = (qa * scale).astype(qa_b.dtype)
    ka = _qk_norm_rope(seg(1), gk_ref[...], cos, sin, ones_ref[...])
    ka_o[...] = ka
    ka_b[...] = ka.astype(BF16)
    va = seg(2)
    va_o[...] = va
    va_b[...] = va.astype(BF16)
    qb_b[...] = (seg(3) * scale).astype(qb_b.dtype)
    kb = seg(4)
    kb_o[...] = kb
    kb_b[...] = kb.astype(BF16)
    vb = seg(5)
    vb_o[...] = vb
    vb_b[...] = vb.astype(BF16)


def _proj(x, g, w_qkv, gq, gk, cos, sin, ones, tm, q_dtype, q_scale):
    n, d = x.shape
    width = w_qkv.shape[1] // 6
    n_pos = cos.shape[0]
    row = pl.BlockSpec((tm, d), lambda i: (i, 0))
    out = pl.BlockSpec((tm, width), lambda i: (i, 0))
    if n_pos == 1:
        pos = _const_spec((1, width))
    else:
        pos = pl.BlockSpec((tm, width), lambda i: (i % (n_pos // tm), 0))
    f32o = jax.ShapeDtypeStruct((n, width), F32)
    b16o = jax.ShapeDtypeStruct((n, width), BF16)
    qo = jax.ShapeDtypeStruct((n, width), q_dtype)
    return pl.pallas_call(
        functools.partial(_proj_body, width=width, scale=q_scale),
        grid=(n // tm,),
        in_specs=[row, _const_spec((1, d)), _const_spec(w_qkv.shape), _const_spec((1, width)),
                  _const_spec((1, width)), pos, pos, _const_spec(ones.shape)],
        out_specs=[out] * 10,
        out_shape=[f32o, f32o, f32o, f32o, qo, b16o, b16o, qo, b16o, b16o],
        compiler_params=_params(("parallel",)),
        name="proj",
    )(x, g, w_qkv, gq, gk, cos, sin, ones)


def _lam(lam_ref, lam_init):
    p = lam_ref[...]
    s1 = jnp.sum(p[0:1] * p[1:2], axis=-1, keepdims=True)
    s2 = jnp.sum(p[2:3] * p[3:4], axis=-1, keepdims=True)
    return jnp.exp(s1) - jnp.exp(s2) + lam_init


def _attn_a_body(q_ref, k_ref, v_ref, lam_ref, gs_ref, o_ref, *, blk, lam_init):
    i = pl.program_id(2)
    q = q_ref[...]
    lane = lax.broadcasted_iota(jnp.int32, q.shape, 1)
    qs = (jnp.where(lane < HEAD_DIM, q, 0), jnp.where(lane >= HEAD_DIM, q, 0))
    rows = lax.broadcasted_iota(jnp.int32, (blk, blk), 0)
    cols = lax.broadcasted_iota(jnp.int32, (blk, blk), 1)

    def step(j, carry, masked):
        off = pl.multiple_of(j * blk, blk)
        k = k_ref[pl.ds(off, blk), :]
        v = v_ref[pl.ds(off, blk), :]
        new = []
        for mp in range(2):
            m, l, acc = carry[3 * mp:3 * mp + 3]
            s = _dot_nt(qs[mp], k)
            if masked:
                s = jnp.where(cols <= rows, s, -jnp.inf)
            m_new = jnp.maximum(m, jnp.max(s, axis=-1, keepdims=True))
            alpha = jnp.exp(m - m_new)
            p = jnp.exp(s - m_new)
            l = alpha * l + jnp.sum(p, axis=-1, keepdims=True)
            acc = alpha * acc + _dot(p.astype(BF16), v)
            new += [m_new, l, acc]
        return tuple(new)

    init = (jnp.full((blk, 1), -jnp.inf, F32), jnp.zeros((blk, 1), F32), jnp.zeros((blk, LANES), F32)) * 2
    carry = lax.fori_loop(0, i, functools.partial(step, masked=False), init)
    _, l1, a1, _, l2, a2 = step(i, carry, True)
    o = a1 / l1 - _lam(lam_ref, lam_init) * (a2 / l2)
    o_ref[...] = (_rms(o, gs_ref[...]) * (1.0 - lam_init)).astype(o_ref.dtype)


def _attn_b_body(q_ref, k_ref, v_ref, tri_ref, o_ref, *, blk):
    i = pl.program_id(2)
    q = q_ref[...]
    lane = lax.broadcasted_iota(jnp.int32, q.shape, 1)
    qs = (jnp.where(lane < HEAD_DIM, q, 0), jnp.where(lane >= HEAD_DIM, q, 0))
    rows = lax.broadcasted_iota(jnp.int32, (blk, blk), 0)
    cols = lax.broadcasted_iota(jnp.int32, (blk, blk), 1)
    tri = tri_ref[...]

    def step(j, carry, masked):
        off = pl.multiple_of(j * blk, blk)
        k = k_ref[pl.ds(off, blk), :]
        v = v_ref[pl.ds(off, blk), :]
        new = []
        for hd in range(2):
            c, acc = carry[2 * hd:2 * hd + 2]
            z = _dot_nt(qs[hd], k)
            log_beta = jnp.minimum(z, 0.0) - jnp.log1p(jnp.exp(-jnp.abs(z)))
            log_1mb = log_beta - z
            if masked:
                log_1mb = jnp.where(cols < rows, log_1mb, 0.0)
            suffix = _split_dot(log_1mb, tri)
            a = jnp.exp(log_beta + suffix + c)
            if masked:
                a = jnp.where(cols < rows, a, 0.0)
            acc = acc + _dot(a.astype(BF16), v)
            c = c + jnp.sum(log_1mb, axis=-1, keepdims=True)
            new += [c, acc]
        return tuple(new)

    init = (jnp.zeros((blk, 1), F32), jnp.zeros((blk, LANES), F32)) * 2
    carry = step(i, init, True)
    carry = lax.fori_loop(0, i, lambda jj, cr: step(i - 1 - jj, cr, False), carry)
    o_ref[...] = jnp.where(lane < HEAD_DIM, carry[1], carry[3]).astype(o_ref.dtype)


def _prompt_attention(body, q, k, v, extra, extra_specs, batch, seq, name):
    n, width = q.shape
    blk = ATTN_BLOCK
    nq = seq // blk
    qspec = pl.BlockSpec((blk, LANES), lambda b, h, i: (b * nq + i, h))
    kvspec = pl.BlockSpec((seq, LANES), lambda b, h, i: (b, h))
    return pl.pallas_call(
        body,
        grid=(batch, width // LANES, nq),
        in_specs=[qspec, kvspec, kvspec] + extra_specs,
        out_specs=qspec,
        out_shape=jax.ShapeDtypeStruct((n, width), BF16),
        compiler_params=_params(("parallel", "parallel", "arbitrary")),
        name=name,
    )(q, k, v, *extra)


def _group_rows(x_row, n_rows):
    w = x_row.shape[1]
    r = lax.broadcasted_iota(jnp.int32, (n_rows, w), 0)
    c = lax.broadcasted_iota(jnp.int32, (n_rows, w), 1)
    return jnp.where(c // HEAD_DIM == r, jnp.broadcast_to(x_row, (n_rows, w)), 0.0)


def _decode_body(pt_ref, qa_ref, ka_ref, va_ref, qb_ref, lam_ref, gs_ref, *rest,
                 n_pages, page, lam_init):
    del pt_ref
    pages = rest[:4 * n_pages]
    oa_ref, ob_ref, m_scr, l_scr, acca_scr, c_scr, accb_scr = rest[4 * n_pages:]
    g = pl.program_id(1)
    rows = 8
    scale = HEAD_DIM ** -0.5
    qa_rows = _group_rows(qa_ref[...], rows)
    qa_bf = (qa_rows * scale).astype(BF16)
    qb_bf = (_group_rows(qb_ref[...], rows) * scale).astype(BF16)

    @pl.when(g == 0)
    def _():
        s_self = jnp.sum(qa_rows * ka_ref[...], axis=-1, keepdims=True) * scale
        m_scr[...] = jnp.broadcast_to(s_self, m_scr.shape)
        l_scr[...] = jnp.ones(l_scr.shape, F32)
        acca_scr[...] = jnp.broadcast_to(va_ref[...], acca_scr.shape)
        c_scr[...] = jnp.zeros(c_scr.shape, F32)
        accb_scr[...] = jnp.zeros(accb_scr.shape, F32)

    m = m_scr[:, 0:1]
    l = l_scr[:, 0:1]
    acc_a = acca_scr[...]
    c = c_scr[:, 0:1]
    acc_b = accb_scr[...]
    lane = lax.broadcasted_iota(jnp.int32, (rows, page), 1)
    for p in range(n_pages):
        ka_p, va_p, kb_p, vb_p = (pages[t * n_pages + p] for t in range(4))
        n_heads = acc_a.shape[1] // LANES
        va = jnp.concatenate([va_p[pl.ds(h, page, stride=n_heads), :] for h in range(n_heads)], axis=1)
        s = _dot(qa_bf, ka_p[...].astype(BF16))
        m_new = jnp.maximum(m, jnp.max(s, axis=-1, keepdims=True))
        alpha = jnp.exp(m - m_new)
        pr = jnp.exp(s - m_new)
        l = alpha * l + jnp.sum(pr, axis=-1, keepdims=True)
        acc_a = alpha * acc_a + _dot(pr.astype(BF16), va.astype(BF16))
        m = m_new
        z = _dot(qb_bf, kb_p[...].astype(BF16))
        log_beta = jnp.minimum(z, 0.0) - jnp.log1p(jnp.exp(-jnp.abs(z)))
        log_1mb = log_beta - z
        incl = log_1mb
        shift = 1
        while shift < page:
            incl = incl + jnp.where(lane + shift < page, pltpu.roll(incl, page - shift, axis=1), 0.0)
            shift *= 2
        a = jnp.exp(log_beta + (incl - log_1mb) + c)
        acc_b = acc_b + _dot_nt(a.astype(BF16), vb_p[...].astype(BF16))
        c = c + incl[:, 0:1]
    m_scr[...] = jnp.broadcast_to(m, m_scr.shape)
    l_scr[...] = jnp.broadcast_to(l, l_scr.shape)
    acca_scr[...] = acc_a
    c_scr[...] = jnp.broadcast_to(c, c_scr.shape)
    accb_scr[...] = acc_b

    @pl.when(g == pl.num_programs(1) - 1)
    def _():
        w = acc_a.shape[1]
        r = lax.broadcasted_iota(jnp.int32, (rows, w), 0)
        col = lax.broadcasted_iota(jnp.int32, (rows, w), 1)
        an = acc_a / l
        own_head = col // LANES == r // 2
        o1 = jnp.sum(jnp.where(own_head & (r % 2 == 0), an, 0.0), axis=0, keepdims=True)
        o2 = jnp.sum(jnp.where(own_head & (r % 2 == 1), an, 0.0), axis=0, keepdims=True)
        o = o1 - _lam(lam_ref, lam_init) * o2
        heads = [_rms(o[:, h * LANES:(h + 1) * LANES], gs_ref[...]) for h in range(w // LANES)]
        oa_ref[...] = jnp.concatenate(heads, axis=1) * (1.0 - lam_init)
        ob_ref[...] = jnp.sum(jnp.where(col // HEAD_DIM == r, acc_b, 0.0), axis=0, keepdims=True)


def _decode(page_table, qa, ka, va, qb, lam_p, gs, caches, lam_init):
    nb, width = qa.shape
    n_tab = page_table.shape[1]
    page = caches[0].shape[2]
    assert all(cch.shape[1:] == (width, page) for cch in caches) and page == LANES
    npg = PAGES_PER_STEP
    steps = n_tab // npg

    def vec(x):
        return x.reshape(nb, 1, width)

    vspec = pl.BlockSpec((None, 1, width), lambda b, g, pt: (b, 0, 0))

    def page_spec(p):
        return pl.BlockSpec((None, width, page),
                            lambda b, g, pt: (pt[b * n_tab + n_tab - 1 - (g * npg + p)], 0, 0))

    page_specs = [page_spec(p) for _ in range(4) for p in range(npg)]
    page_args = [cch for cch in caches for _ in range(npg)]
    o_sds = jax.ShapeDtypeStruct((nb, 1, width), F32)
    grid_spec = pltpu.PrefetchScalarGridSpec(
        num_scalar_prefetch=1,
        grid=(nb, steps),
        in_specs=[vspec, vspec, vspec, vspec, _const_spec(lam_p.shape), _const_spec(gs.shape)] + page_specs,
        out_specs=[vspec, vspec],
        scratch_shapes=[pltpu.VMEM((8, LANES), F32), pltpu.VMEM((8, LANES), F32), pltpu.VMEM((8, width), F32),
                        pltpu.VMEM((8, LANES), F32), pltpu.VMEM((8, width), F32)],
    )
    oa, ob = pl.pallas_call(
        functools.partial(_decode_body, n_pages=npg, page=page, lam_init=lam_init),
        grid_spec=grid_spec,
        out_shape=[o_sds, o_sds],
        compiler_params=_params(("parallel", "arbitrary")),
        name="decode",
    )(page_table.reshape(-1), vec(qa), vec(ka), vec(va), vec(qb), lam_p, gs, *page_args)
    return oa.reshape(nb, width), ob.reshape(nb, width)


def _merge_body(x_ref, oa_ref, ob_ref, gm_ref, wg_ref, woa_ref, wob_ref, wout_ref,
                g2_ref, wup_ref, wdn_ref, o_ref, h_scr, *, dff):
    x = x_ref[...]
    d = x.shape[1]
    h = _rms(x, gm_ref[...]).astype(BF16)
    gate_a = 1.0 / (1.0 + jnp.exp(-_dot(h, wg_ref[:, :d])))
    gate_b = 1.0 / (1.0 + jnp.exp(-_dot(h, wg_ref[:, d:])))
    merged = (gate_a * _dot(oa_ref[...].astype(BF16), woa_ref[...])
              + gate_b * _dot(ob_ref[...].astype(BF16), wob_ref[...]))
    x = x + _dot(merged.astype(BF16), wout_ref[...])
    _swiglu_into(_rms(x, g2_ref[...]).astype(BF16), wup_ref, h_scr, dff)
    o_ref[...] = x + 0.5 * _dot(h_scr[...], wdn_ref[...])


def _merge(x, oa, ob, gm, wg, woa, wob, wout, g2, wup, wdn, tm):
    n, d = x.shape
    dff = wdn.shape[0]
    row = pl.BlockSpec((tm, d), lambda i: (i, 0))
    mix = pl.BlockSpec((tm, oa.shape[1]), lambda i: (i, 0))
    consts = [_const_spec(a.shape) for a in (gm, wg, woa, wob, wout, g2, wup, wdn)]
    return pl.pallas_call(
        functools.partial(_merge_body, dff=dff),
        grid=(n // tm,),
        in_specs=[row, mix, mix] + consts,
        out_specs=row,
        out_shape=jax.ShapeDtypeStruct((n, d), F32),
        scratch_shapes=[pltpu.VMEM((tm, dff), BF16)],
        compiler_params=_params(("parallel",)),
        name="merge",
    )(x, oa, ob, gm, wg, woa, wob, wout, g2, wup, wdn)


def _rope_tables(pos, n_groups):
    half = HEAD_DIM // 2
    inv = ROPE_THETA ** (-jnp.arange(half, dtype=F32) / half)
    ang = pos.astype(F32)[:, None] * inv[None, :]
    ang = jnp.concatenate([ang, ang], axis=-1)
    sign = jnp.concatenate([-jnp.ones((half,), F32), jnp.ones((half,), F32)])
    return jnp.tile(jnp.cos(ang), (1, n_groups)), jnp.tile(jnp.sin(ang) * sign, (1, n_groups))


def kernel(x_prompt, x_sample, cache_a_k, cache_a_v, cache_b_k, cache_b_v, page_table, g_ffn1, w_ffn1_up, w_ffn1_down, g_mix, w_in, g_q_a, g_k_a, lambda_q1, lambda_k1, lambda_q2, lambda_k2, g_subln_a, w_o_a, w_o_b, w_out, g_ffn2, w_ffn2_up, w_ffn2_down):
    batch, seq, d = x_prompt.shape
    nb, dec_seq, _ = x_sample.shape
    depth, n_phys, page = cache_a_k.shape[:3]
    assert depth == 1 and dec_seq == 1
    width = w_o_a.shape[1]
    n_groups = width // HEAD_DIM
    past_len = page_table.shape[1] * page
    lam_init = _lambda_init(0)

    def row(v):
        return v.reshape(1, -1)

    w_qkv = w_in[0, :, :6 * width].astype(BF16)
    w_gate = w_in[0, :, 6 * width:].astype(BF16)
    wup1, wdn1 = w_ffn1_up[0].astype(BF16), w_ffn1_down[0].astype(BF16)
    wup2, wdn2 = w_ffn2_up[0].astype(BF16), w_ffn2_down[0].astype(BF16)
    woa, wob, wout = w_o_a[0].astype(BF16), w_o_b[0].astype(BF16), w_out[0].astype(BF16)
    gq = jnp.tile(row(g_q_a[0]), (1, n_groups))
    gk = jnp.tile(row(g_k_a[0]), (1, n_groups))
    gs = row(g_subln_a[0])
    lam_p = jnp.stack([lambda_q1[0], lambda_k1[0], lambda_q2[0], lambda_k2[0]])
    idx = jnp.arange(width, dtype=jnp.int32) // HEAD_DIM
    group_ones = (idx[:, None] == idx[None, :]).astype(BF16)
    blk_idx = jnp.arange(ATTN_BLOCK, dtype=jnp.int32)
    tri = (blk_idx[:, None] > blk_idx[None, :]).astype(BF16)

    def trunk_front(x, pos, tm, q_dtype, q_scale):
        x1 = _ffn(x, row(g_ffn1[0]), wup1, wdn1, tm)
        cos, sin = _rope_tables(pos, n_groups)
        return x1, _proj(x1, row(g_mix[0]), w_qkv, gq, gk, cos, sin, group_ones, tm, q_dtype, q_scale)

    def trunk_back(x1, oa, ob, tm):
        return _merge(x1, oa, ob, row(g_mix[0]), w_gate, woa, wob, wout, row(g_ffn2[0]), wup2, wdn2, tm)

    n = batch * seq
    x1p, (ka_p, va_p, kb_p, vb_p, qa_h, ka_h, va_h, qb_h, kb_h, vb_h) = trunk_front(
        x_prompt.reshape(n, d), jnp.arange(seq, dtype=jnp.int32), 256, BF16, HEAD_DIM ** -0.5)
    oa_p = _prompt_attention(
        functools.partial(_attn_a_body, blk=ATTN_BLOCK, lam_init=lam_init), qa_h, ka_h, va_h,
        (lam_p, gs), [_const_spec(lam_p.shape), _const_spec(gs.shape)], batch, seq, "attn_a")
    ob_p = _prompt_attention(
        functools.partial(_attn_b_body, blk=ATTN_BLOCK), qb_h, kb_h, vb_h,
        (tri,), [_const_spec(tri.shape)], batch, seq, "attn_b")
    y_prompt = trunk_back(x1p, oa_p, ob_p, 256).reshape(batch, seq, d)

    x1s, (ka_s, va_s, kb_s, vb_s, qa_s, _, _, qb_s, _, _) = trunk_front(
        x_sample.reshape(nb, d), past_len + jnp.arange(dec_seq, dtype=jnp.int32), nb, F32, 1.0)
    caches = [jnp.transpose(cache_a_k[0], (0, 2, 3, 4, 1)).reshape(n_phys, width, page),
              cache_a_v[0].reshape(n_phys, page * (width // LANES), LANES),
              jnp.transpose(cache_b_k[0], (0, 2, 3, 1)).reshape(n_phys, width, page),
              jnp.transpose(cache_b_v[0], (0, 2, 3, 1)).reshape(n_phys, width, page)]
    oa_s, ob_s = _decode(page_table, qa_s, ka_s, va_s, qb_s, lam_p, gs, caches, lam_init)
    y_sample = trunk_back(x1s, oa_s, ob_s, nb).reshape(nb, dec_seq, d)

    n_a, n_b = w_o_a.shape[1] // (2 * HEAD_DIM), w_o_b.shape[1] // HEAD_DIM
    return (y_prompt, y_sample,
            ka_p.reshape(1, batch, seq, n_a, 2, HEAD_DIM), va_p.reshape(1, batch, seq, n_a, 2 * HEAD_DIM),
            kb_p.reshape(1, batch, seq, n_b, HEAD_DIM), vb_p.reshape(1, batch, seq, n_b, HEAD_DIM),
            ka_s.reshape(1, nb, dec_seq, n_a, 2, HEAD_DIM), va_s.reshape(1, nb, dec_seq, n_a, 2 * HEAD_DIM),
            kb_s.reshape(1, nb, dec_seq, n_b, HEAD_DIM), vb_s.reshape(1, nb, dec_seq, n_b, HEAD_DIM))
```

```python
import functools
import math

import jax
import jax.numpy as jnp
from jax import lax
from jax.experimental import pallas as pl
from jax.experimental.pallas import tpu as pltpu

F32 = jnp.float32
BF16 = jnp.bfloat16

RMS_EPS = 1e-6
ROPE_THETA = 10000.0
HEAD_DIM = 64
LANES = 128
VMEM_LIMIT_BYTES = 56 * 1024 * 1024
FF_CHUNK = 256
ATTN_BLOCK = 512
SUFFIX_BLOCK = 256
PAGES_PER_STEP = 16


def _lambda_init(layer_idx):
    return 0.8 - 0.6 * math.exp(-0.3 * layer_idx)


def _const_spec(shape):
    nd = len(shape)
    return pl.BlockSpec(shape, lambda *_: (0,) * nd, pipeline_mode=pl.Buffered(1))


def _params(sem):
    return pltpu.CompilerParams(dimension_semantics=sem, vmem_limit_bytes=VMEM_LIMIT_BYTES)


def _rms(x, g):
    return x * lax.rsqrt(jnp.mean(x * x, axis=-1, keepdims=True) + RMS_EPS) * g


def _dot(a, b):
    return jnp.dot(a, b, preferred_element_type=F32)


def _dot_nt(a, b):
    return lax.dot_general(a, b, (((1,), (1,)), ((), ())), preferred_element_type=F32)


def _split_dot(x, w):
    hi = x.astype(BF16)
    lo = (x - hi.astype(F32)).astype(BF16)
    return _dot(hi, w) + _dot(lo, w)


def _log_sigmoid(z):
    return jnp.minimum(z, 0.0) - jnp.log(1.0 + jnp.exp(-jnp.abs(z)))


def _swiglu_into(xb, wup_ref, h_scr, dff):
    for c in range(dff // FF_CHUNK):
        lo = c * FF_CHUNK
        g = _dot(xb, wup_ref[:, lo:lo + FF_CHUNK])
        u = _dot(xb, wup_ref[:, dff + lo:dff + lo + FF_CHUNK])
        h_scr[:, lo:lo + FF_CHUNK] = (g * (1.0 / (1.0 + jnp.exp(-g))) * u).astype(BF16)


def _ffn_body(x_ref, g_ref, wup_ref, wdn_ref, o_ref, h_scr, *, dff):
    x = x_ref[...]
    _swiglu_into(_rms(x, g_ref[...]).astype(BF16), wup_ref, h_scr, dff)
    o_ref[...] = x + 0.5 * _dot(h_scr[...], wdn_ref[...])


def _ffn(x, g, wup, wdn, tm):
    n, d = x.shape
    dff = wdn.shape[0]
    row = pl.BlockSpec((tm, d), lambda i: (i, 0))
    return pl.pallas_call(
        functools.partial(_ffn_body, dff=dff),
        grid=(n // tm,),
        in_specs=[row, _const_spec((1, d)), _const_spec(wup.shape), _const_spec(wdn.shape)],
        out_specs=row,
        out_shape=jax.ShapeDtypeStruct((n, d), F32),
        scratch_shapes=[pltpu.VMEM((tm, dff), BF16)],
        compiler_params=_params(("parallel",)),
        name="ffn",
    )(x, g, wup, wdn)


def _swap_halves(x):
    lane = lax.broadcasted_iota(jnp.int32, (x.shape[0], LANES), 1)
    first = (lane % HEAD_DIM) < (HEAD_DIM // 2)
    out = []
    for c in range(x.shape[1] // LANES):
        xs = x[:, c * LANES:(c + 1) * LANES]
        up = pltpu.roll(xs, LANES - HEAD_DIM // 2, axis=1)
        down = pltpu.roll(xs, HEAD_DIM // 2, axis=1)
        out.append(jnp.where(first, up, down))
    return jnp.concatenate(out, axis=1)


def _qk_norm_rope(x, g, cos, sin_signed, group_ones):
    ms = _split_dot(x * x, group_ones) * (1.0 / HEAD_DIM)
    y = x * lax.rsqrt(ms + RMS_EPS) * g
    return y * cos + _swap_halves(y) * sin_signed


def _proj_body(x_ref, g_ref, w_ref, gq_ref, gk_ref, cos_ref, sin_ref, ones_ref,
               ka_o, va_o, kb_o, vb_o, qa_b, ka_b, va_b, qb_b, kb_b, vb_b, *, width, scale, feature_major):
    h = _rms(x_ref[...], g_ref[...]).astype(BF16)

    def seg(i):
        return _dot(h, w_ref[:, i * width:(i + 1) * width])

    def stored(x):
        return x.T if feature_major else x

    cos, sin = cos_ref[...], sin_ref[...]
    qa = _qk_norm_rope(seg(0), gq_ref[...], cos, sin, ones_ref[...])
    qa_b[...] = (qa * scale).astype(qa_b.dtype)
    ka = _qk_norm_rope(seg(1), gk_ref[...], cos, sin, ones_ref[...])
    ka_o[...] = stored(ka)
    ka_b[...] = ka.astype(BF16)
    va = seg(2)
    va_o[...] = va
    va_b[...] = va.astype(BF16)
    qb_b[...] = (seg(3) * scale).astype(qb_b.dtype)
    kb = seg(4)
    kb_o[...] = stored(kb)
    kb_b[...] = kb.astype(BF16)
    vb = seg(5)
    vb_o[...] = stored(vb)
    vb_b[...] = vb.astype(BF16)


def _proj(x, g, w_qkv, gq, gk, cos, sin, ones, tm, q_dtype, q_scale, seq=None):
    n, d = x.shape
    width = w_qkv.shape[1] // 6
    n_pos = cos.shape[0]
    row = pl.BlockSpec((tm, d), lambda i: (i, 0))
    out = pl.BlockSpec((tm, width), lambda i: (i, 0))
    if n_pos == 1:
        pos = _const_spec((1, width))
    else:
        pos = pl.BlockSpec((tm, width), lambda i: (i % (n_pos // tm), 0))
    f32o = jax.ShapeDtypeStruct((n, width), F32)
    if seq is None:
        fm_spec, fm_o = out, f32o
    else:
        per_seq = seq // tm
        fm_spec = pl.BlockSpec((None, width, tm), lambda i: (i // per_seq, 0, i % per_seq))
        fm_o = jax.ShapeDtypeStruct((n // seq, width, seq), F32)
    b16o = jax.ShapeDtypeStruct((n, width), BF16)
    qo = jax.ShapeDtypeStruct((n, width), q_dtype)
    return pl.pallas_call(
        functools.partial(_proj_body, width=width, scale=q_scale, feature_major=seq is not None),
        grid=(n // tm,),
        in_specs=[row, _const_spec((1, d)), _const_spec(w_qkv.shape), _const_spec((1, width)),
                  _const_spec((1, width)), pos, pos, _const_spec(ones.shape)],
        out_specs=[fm_spec, out, fm_spec, fm_spec] + [out] * 6,
        out_shape=[fm_o, f32o, fm_o, fm_o, qo, b16o, b16o, qo, b16o, b16o],
        compiler_params=_params(("parallel",)),
        name="proj",
    )(x, g, w_qkv, gq, gk, cos, sin, ones)


def _lam(lam_ref, lam_init):
    p = lam_ref[...]
    s1 = jnp.sum(p[0:1] * p[1:2], axis=-1, keepdims=True)
    s2 = jnp.sum(p[2:3] * p[3:4], axis=-1, keepdims=True)
    return jnp.exp(s1) - jnp.exp(s2) + lam_init


def _attn_a_body(q_ref, k_ref, v_ref, lam_ref, gs_ref, o_ref, *, blk, lam_init):
    i = pl.program_id(2)
    q = q_ref[...]
    lane = lax.broadcasted_iota(jnp.int32, q.shape, 1)
    qs = (jnp.where(lane < HEAD_DIM, q, 0), jnp.where(lane >= HEAD_DIM, q, 0))
    rows = lax.broadcasted_iota(jnp.int32, (blk, blk), 0)
    cols = lax.broadcasted_iota(jnp.int32, (blk, blk), 1)

    def step(j, carry, masked):
        off = pl.multiple_of(j * blk, blk)
        k = k_ref[pl.ds(off, blk), :]
        v = v_ref[pl.ds(off, blk), :]
        new = []
        for mp in range(2):
            m, l, acc = carry[3 * mp:3 * mp + 3]
            s = _dot_nt(qs[mp], k)
            if masked:
                s = jnp.where(cols <= rows, s, -jnp.inf)
            m_new = jnp.maximum(m, jnp.max(s, axis=-1, keepdims=True))
            alpha = jnp.exp(m - m_new)
            p = jnp.exp(s - m_new)
            l = alpha * l + jnp.sum(p, axis=-1, keepdims=True)
            acc = alpha * acc + _dot(p.astype(BF16), v)
            new += [m_new, l, acc]
        return tuple(new)

    init = (jnp.full((blk, 1), -jnp.inf, F32), jnp.zeros((blk, 1), F32), jnp.zeros((blk, LANES), F32)) * 2
    carry = lax.fori_loop(0, i, functools.partial(step, masked=False), init)
    _, l1, a1, _, l2, a2 = step(i, carry, True)
    o = a1 / l1 - _lam(lam_ref, lam_init) * (a2 / l2)
    o_ref[...] = (_rms(o, gs_ref[...]) * (1.0 - lam_init)).astype(o_ref.dtype)


def _attn_b_body(q_ref, k_ref, v_ref, tri_ref, o_ref, *, blk):
    i = pl.program_id(2)
    q = q_ref[...]
    lane = lax.broadcasted_iota(jnp.int32, q.shape, 1)
    qs = (jnp.where(lane < HEAD_DIM, q, 0), jnp.where(lane >= HEAD_DIM, q, 0))
    rows = lax.broadcasted_iota(jnp.int32, (blk, blk), 0)
    cols = lax.broadcasted_iota(jnp.int32, (blk, blk), 1)
    tri = tri_ref[...]

    def step(j, carry, masked):
        off = pl.multiple_of(j * blk, blk)
        k = k_ref[pl.ds(off, blk), :]
        v = v_ref[pl.ds(off, blk), :]
        new = []
        for hd in range(2):
            c, acc = carry[2 * hd:2 * hd + 2]
            z = _dot_nt(qs[hd], k)
            log_beta = _log_sigmoid(z)
            log_1mb = log_beta - z
            if masked:
                log_1mb = jnp.where(cols < rows, log_1mb, 0.0)
            parts = [None] * (blk // SUFFIX_BLOCK)
            for sb in reversed(range(blk // SUFFIX_BLOCK)):
                sl = slice(sb * SUFFIX_BLOCK, (sb + 1) * SUFFIX_BLOCK)
                suffix = _split_dot(log_1mb[:, sl], tri)
                parts[sb] = jnp.exp(log_beta[:, sl] + suffix + c)
                c = c + jnp.sum(log_1mb[:, sl], axis=-1, keepdims=True)
            a = jnp.concatenate(parts, axis=1)
            if masked:
                a = jnp.where(cols < rows, a, 0.0)
            acc = acc + _dot(a.astype(BF16), v)
            new += [c, acc]
        return tuple(new)

    init = (jnp.zeros((blk, 1), F32), jnp.zeros((blk, LANES), F32)) * 2
    carry = step(i, init, True)
    carry = lax.fori_loop(0, i, lambda jj, cr: step(i - 1 - jj, cr, False), carry)
    o_ref[...] = jnp.where(lane < HEAD_DIM, carry[1], carry[3]).astype(o_ref.dtype)


def _prompt_attention(body, q, k, v, extra, extra_specs, batch, seq, name):
    n, width = q.shape
    blk = ATTN_BLOCK
    nq = seq // blk
    qspec = pl.BlockSpec((blk, LANES), lambda b, h, i: (b * nq + i, h))
    kvspec = pl.BlockSpec((seq, LANES), lambda b, h, i: (b, h))
    return pl.pallas_call(
        body,
        grid=(batch, width // LANES, nq),
        in_specs=[qspec, kvspec, kvspec] + extra_specs,
        out_specs=qspec,
        out_shape=jax.ShapeDtypeStruct((n, width), BF16),
        compiler_params=_params(("parallel", "parallel", "arbitrary")),
        name=name,
    )(q, k, v, *extra)


def _group_rows(x_row, n_rows):
    w = x_row.shape[1]
    r = lax.broadcasted_iota(jnp.int32, (n_rows, w), 0)
    c = lax.broadcasted_iota(jnp.int32, (n_rows, w), 1)
    return jnp.where(c // HEAD_DIM == r, jnp.broadcast_to(x_row, (n_rows, w)), 0.0)


def _decode_body(pt_ref, qa_ref, ka_ref, va_ref, qb_ref, lam_ref, gs_ref, tri_ref, *rest,
                 n_pages, page, lam_init):
    del pt_ref
    pages = rest[:4 * n_pages]
    oa_ref, ob_ref, m_scr, l_scr, acca_scr, c_scr, accb_scr = rest[4 * n_pages:]
    g = pl.program_id(1)
    rows = 8
    scale = HEAD_DIM ** -0.5
    qa_rows = _group_rows(qa_ref[...], rows)
    qa_bf = (qa_rows * scale).astype(BF16)
    qb_bf = (_group_rows(qb_ref[...], rows) * scale).astype(BF16)

    @pl.when(g == 0)
    def _():
        s_self = jnp.sum(qa_rows * ka_ref[...], axis=-1, keepdims=True) * scale
        m_scr[...] = jnp.broadcast_to(s_self, m_scr.shape)
        l_scr[...] = jnp.ones(l_scr.shape, F32)
        acca_scr[...] = jnp.broadcast_to(va_ref[...], acca_scr.shape)
        c_scr[...] = jnp.zeros(c_scr.shape, F32)
        accb_scr[...] = jnp.zeros(accb_scr.shape, F32)

    ka_pages, va_pages, kb_pages, vb_pages = (pages[t * n_pages:(t + 1) * n_pages] for t in range(4))
    n_heads = acca_scr.shape[1] // LANES

    s = jnp.concatenate([_dot(qa_bf, kp[...].astype(BF16)) for kp in ka_pages], axis=1)
    m = m_scr[...]
    m_new = jnp.maximum(m, jnp.max(s, axis=-1, keepdims=True))
    alpha = jnp.exp(m - m_new)[:, 0:1]
    pr = jnp.exp(s - m_new[:, 0:1])
    l_scr[...] = alpha * l_scr[...] + jnp.sum(pr, axis=-1, keepdims=True)
    m_scr[...] = m_new
    pr = pr.astype(BF16)
    acc_a = alpha * acca_scr[...]
    for p, vp in enumerate(va_pages):
        va = jnp.concatenate([vp[pl.ds(h, page, stride=n_heads), :] for h in range(n_heads)], axis=1)
        acc_a = acc_a + _dot(pr[:, p * page:(p + 1) * page], va.astype(BF16))
    acca_scr[...] = acc_a

    z = jnp.concatenate([_dot(qb_bf, kp[...].astype(BF16)) for kp in kb_pages], axis=1)
    log_beta = _log_sigmoid(z)
    log_1mb = log_beta - z
    hi = log_1mb.astype(BF16).astype(F32)
    hi_lo = jnp.concatenate([hi, log_1mb - hi], axis=0).astype(BF16)
    stacked = jnp.concatenate([hi_lo[:, p * page:(p + 1) * page] for p in range(n_pages)], axis=0)
    both = _dot(stacked, tri_ref[...])
    c = c_scr[...]
    weights = []
    for p in range(n_pages):
        bp = both[2 * rows * p:2 * rows * p + rows] + both[2 * rows * p + rows:2 * rows * (p + 1)]
        weights.append(jnp.exp(log_beta[:, p * page:(p + 1) * page] + bp[:, :page] + c).astype(BF16))
        c = c + bp[:, page:]
    c_scr[...] = c
    acc_b = accb_scr[...]
    for a, vp in zip(weights, vb_pages):
        acc_b = acc_b + _dot_nt(a, vp[...].astype(BF16))
    accb_scr[...] = acc_b

    @pl.when(g == pl.num_programs(1) - 1)
    def _():
        w = acc_a.shape[1]
        r = lax.broadcasted_iota(jnp.int32, (rows, w), 0)
        col = lax.broadcasted_iota(jnp.int32, (rows, w), 1)
        an = acc_a / l_scr[:, 0:1]
        own_head = col // LANES == r // 2
        o1 = jnp.sum(jnp.where(own_head & (r % 2 == 0), an, 0.0), axis=0, keepdims=True)
        o2 = jnp.sum(jnp.where(own_head & (r % 2 == 1), an, 0.0), axis=0, keepdims=True)
        o = o1 - _lam(lam_ref, lam_init) * o2
        heads = [_rms(o[:, h * LANES:(h + 1) * LANES], gs_ref[...]) for h in range(w // LANES)]
        oa_ref[...] = jnp.concatenate(heads, axis=1) * (1.0 - lam_init)
        ob_ref[...] = jnp.sum(jnp.where(col // HEAD_DIM == r, acc_b, 0.0), axis=0, keepdims=True)


def _decode(page_table, qa, ka, va, qb, lam_p, gs, tri_ones, caches, lam_init):
    nb, width = qa.shape
    n_tab = page_table.shape[1]
    page = caches[0].shape[2]
    assert all(cch.shape[1:] == (width, page) for cch in caches) and page == LANES
    npg = PAGES_PER_STEP
    steps = n_tab // npg

    def vec(x):
        return x.reshape(nb, 1, width)

    vspec = pl.BlockSpec((None, 1, width), lambda b, g, pt: (b, 0, 0))

    def page_spec(p):
        return pl.BlockSpec((None, width, page),
                            lambda b, g, pt: (pt[b * n_tab + n_tab - 1 - (g * npg + p)], 0, 0))

    page_specs = [page_spec(p) for _ in range(4) for p in range(npg)]
    page_args = [cch for cch in caches for _ in range(npg)]
    o_sds = jax.ShapeDtypeStruct((nb, 1, width), F32)
    grid_spec = pltpu.PrefetchScalarGridSpec(
        num_scalar_prefetch=1,
        grid=(nb, steps),
        in_specs=[vspec, vspec, vspec, vspec, _const_spec(lam_p.shape), _const_spec(gs.shape),
                  _const_spec(tri_ones.shape)] + page_specs,
        out_specs=[vspec, vspec],
        scratch_shapes=[pltpu.VMEM((8, LANES), F32), pltpu.VMEM((8, LANES), F32), pltpu.VMEM((8, width), F32),
                        pltpu.VMEM((8, LANES), F32), pltpu.VMEM((8, width), F32)],
    )
    oa, ob = pl.pallas_call(
        functools.partial(_decode_body, n_pages=npg, page=page, lam_init=lam_init),
        grid_spec=grid_spec,
        out_shape=[o_sds, o_sds],
        compiler_params=_params(("parallel", "arbitrary")),
        name="decode",
    )(page_table.reshape(-1), vec(qa), vec(ka), vec(va), vec(qb), lam_p, gs, tri_ones, *page_args)
    return oa.reshape(nb, width), ob.reshape(nb, width)


def _merge_body(x_ref, oa_ref, ob_ref, gm_ref, wg_ref, woa_ref, wob_ref, wout_ref,
                g2_ref, wup_ref, wdn_ref, o_ref, h_scr, *, dff):
    x = x_ref[...]
    d = x.shape[1]
    h = _rms(x, gm_ref[...]).astype(BF16)
    gate_a = 1.0 / (1.0 + jnp.exp(-_dot(h, wg_ref[:, :d])))
    gate_b = 1.0 / (1.0 + jnp.exp(-_dot(h, wg_ref[:, d:])))
    merged = (gate_a * _dot(oa_ref[...].astype(BF16), woa_ref[...])
              + gate_b * _dot(ob_ref[...].astype(BF16), wob_ref[...]))
    x = x + _dot(merged.astype(BF16), wout_ref[...])
    _swiglu_into(_rms(x, g2_ref[...]).astype(BF16), wup_ref, h_scr, dff)
    o_ref[...] = x + 0.5 * _dot(h_scr[...], wdn_ref[...])


def _merge(x, oa, ob, gm, wg, woa, wob, wout, g2, wup, wdn, tm):
    n, d = x.shape
    dff = wdn.shape[0]
    row = pl.BlockSpec((tm, d), lambda i: (i, 0))
    mix = pl.BlockSpec((tm, oa.shape[1]), lambda i: (i, 0))
    consts = [_const_spec(a.shape) for a in (gm, wg, woa, wob, wout, g2, wup, wdn)]
    return pl.pallas_call(
        functools.partial(_merge_body, dff=dff),
        grid=(n // tm,),
        in_specs=[row, mix, mix] + consts,
        out_specs=row,
        out_shape=jax.ShapeDtypeStruct((n, d), F32),
        scratch_shapes=[pltpu.VMEM((tm, dff), BF16)],
        compiler_params=_params(("parallel",)),
        name="merge",
    )(x, oa, ob, gm, wg, woa, wob, wout, g2, wup, wdn)


def _rope_tables(pos, n_groups):
    half = HEAD_DIM // 2
    inv = ROPE_THETA ** (-jnp.arange(half, dtype=F32) / half)
    ang = pos.astype(F32)[:, None] * inv[None, :]
    ang = jnp.concatenate([ang, ang], axis=-1)
    sign = jnp.concatenate([-jnp.ones((half,), F32), jnp.ones((half,), F32)])
    return jnp.tile(jnp.cos(ang), (1, n_groups)), jnp.tile(jnp.sin(ang) * sign, (1, n_groups))


def kernel(x_prompt, x_sample, cache_a_k, cache_a_v, cache_b_k, cache_b_v, page_table, g_ffn1, w_ffn1_up, w_ffn1_down, g_mix, w_in, g_q_a, g_k_a, lambda_q1, lambda_k1, lambda_q2, lambda_k2, g_subln_a, w_o_a, w_o_b, w_out, g_ffn2, w_ffn2_up, w_ffn2_down):
    batch, seq, d = x_prompt.shape
    nb, dec_seq, _ = x_sample.shape
    depth, n_phys, page = cache_a_k.shape[:3]
    assert depth == 1 and dec_seq == 1
    width = w_o_a.shape[1]
    n_groups = width // HEAD_DIM
    past_len = page_table.shape[1] * page
    lam_init = _lambda_init(0)

    def row(v):
        return v.reshape(1, -1)

    w_qkv = w_in[0, :, :6 * width].astype(BF16)
    w_gate = w_in[0, :, 6 * width:].astype(BF16)
    wup1, wdn1 = w_ffn1_up[0].astype(BF16), w_ffn1_down[0].astype(BF16)
    wup2, wdn2 = w_ffn2_up[0].astype(BF16), w_ffn2_down[0].astype(BF16)
    woa, wob, wout = w_o_a[0].astype(BF16), w_o_b[0].astype(BF16), w_out[0].astype(BF16)
    gq = jnp.tile(row(g_q_a[0]), (1, n_groups))
    gk = jnp.tile(row(g_k_a[0]), (1, n_groups))
    gs = row(g_subln_a[0])
    lam_p = jnp.stack([lambda_q1[0], lambda_k1[0], lambda_q2[0], lambda_k2[0]])
    idx = jnp.arange(width, dtype=jnp.int32) // HEAD_DIM
    group_ones = (idx[:, None] == idx[None, :]).astype(BF16)
    blk_idx = jnp.arange(SUFFIX_BLOCK, dtype=jnp.int32)
    tri = (blk_idx[:, None] > blk_idx[None, :]).astype(BF16)

    def trunk_front(x, pos, tm, q_dtype, q_scale, seq=None):
        x1 = _ffn(x, row(g_ffn1[0]), wup1, wdn1, tm)
        cos, sin = _rope_tables(pos, n_groups)
        return x1, _proj(x1, row(g_mix[0]), w_qkv, gq, gk, cos, sin, group_ones, tm, q_dtype, q_scale, seq)

    def trunk_back(x1, oa, ob, tm):
        return _merge(x1, oa, ob, row(g_mix[0]), w_gate, woa, wob, wout, row(g_ffn2[0]), wup2, wdn2, tm)

    n = batch * seq
    x1p, (ka_p, va_p, kb_p, vb_p, qa_h, ka_h, va_h, qb_h, kb_h, vb_h) = trunk_front(
        x_prompt.reshape(n, d), jnp.arange(seq, dtype=jnp.int32), 256, BF16, HEAD_DIM ** -0.5, seq)
    oa_p = _prompt_attention(
        functools.partial(_attn_a_body, blk=ATTN_BLOCK, lam_init=lam_init), qa_h, ka_h, va_h,
        (lam_p, gs), [_const_spec(lam_p.shape), _const_spec(gs.shape)], batch, seq, "attn_a")
    ob_p = _prompt_attention(
        functools.partial(_attn_b_body, blk=ATTN_BLOCK), qb_h, kb_h, vb_h,
        (tri,), [_const_spec(tri.shape)], batch, seq, "attn_b")
    y_prompt = trunk_back(x1p, oa_p, ob_p, 256).reshape(batch, seq, d)

    x1s, (ka_s, va_s, kb_s, vb_s, qa_s, _, _, qb_s, _, _) = trunk_front(
        x_sample.reshape(nb, d), past_len + jnp.arange(dec_seq, dtype=jnp.int32), nb, F32, 1.0)
    caches = [jnp.transpose(cache_a_k[0], (0, 2, 3, 4, 1)).reshape(n_phys, width, page),
              cache_a_v[0].reshape(n_phys, page * (width // LANES), LANES),
              jnp.transpose(cache_b_k[0], (0, 2, 3, 1)).reshape(n_phys, width, page),
              jnp.transpose(cache_b_v[0], (0, 2, 3, 1)).reshape(n_phys, width, page)]
    pg_idx = jnp.arange(page, dtype=jnp.int32)
    tri_ones = jnp.concatenate([(pg_idx[:, None] > pg_idx[None, :]).astype(BF16), jnp.ones((page, page), BF16)], axis=1)
    oa_s, ob_s = _decode(page_table, qa_s, ka_s, va_s, qb_s, lam_p, gs, tri_ones, caches, lam_init)
    y_sample = trunk_back(x1s, oa_s, ob_s, nb).reshape(nb, dec_seq, d)

    n_a, n_b = w_o_a.shape[1] // (2 * HEAD_DIM), w_o_b.shape[1] // HEAD_DIM
    return (y_prompt, y_sample,
            jnp.transpose(ka_p.reshape(1, batch, n_a, 2, HEAD_DIM, seq), (0, 1, 5, 2, 3, 4)),
            va_p.reshape(1, batch, seq, n_a, 2 * HEAD_DIM),
            jnp.transpose(kb_p.reshape(1, batch, n_b, HEAD_DIM, seq), (0, 1, 4, 2, 3)),
            jnp.transpose(vb_p.reshape(1, batch, n_b, HEAD_DIM, seq), (0, 1, 4, 2, 3)),
            ka_s.reshape(1, nb, dec_seq, n_a, 2, HEAD_DIM), va_s.reshape(1, nb, dec_seq, n_a, 2 * HEAD_DIM),
            kb_s.reshape(1, nb, dec_seq, n_b, HEAD_DIM), vb_s.reshape(1, nb, dec_seq, n_b, HEAD_DIM))
```

```python
import functools
import math

import jax
import jax.numpy as jnp
from jax import lax
from jax.experimental import pallas as pl
from jax.experimental.pallas import tpu as pltpu

F32 = jnp.float32
BF16 = jnp.bfloat16

RMS_EPS = 1e-6
ROPE_THETA = 10000.0
HEAD_DIM = 64
LANES = 128
VMEM_LIMIT_BYTES = 56 * 1024 * 1024
FF_CHUNK = 256
ATTN_BLOCK = 512
SUFFIX_BLOCK = 256
PAGES_PER_STEP = 16


def _lambda_init(layer_idx):
    return 0.8 - 0.6 * math.exp(-0.3 * layer_idx)


def _const_spec(shape):
    nd = len(shape)
    return pl.BlockSpec(shape, lambda *_: (0,) * nd, pipeline_mode=pl.Buffered(1))


def _params(sem):
    return pltpu.CompilerParams(dimension_semantics=sem, vmem_limit_bytes=VMEM_LIMIT_BYTES)


def _rms(x, g):
    return x * lax.rsqrt(jnp.mean(x * x, axis=-1, keepdims=True) + RMS_EPS) * g


def _dot(a, b):
    return jnp.dot(a, b, preferred_element_type=F32)


def _dot_nt(a, b):
    return lax.dot_general(a, b, (((1,), (1,)), ((), ())), preferred_element_type=F32)


def _split_dot(x, w):
    hi = x.astype(BF16)
    lo = (x - hi.astype(F32)).astype(BF16)
    return _dot(hi, w) + _dot(lo, w)


def _log_sigmoid(z):
    return jnp.minimum(z, 0.0) - jnp.log(1.0 + jnp.exp(-jnp.abs(z)))


def _swiglu_into(xb, wup_ref, h_scr, dff):
    for c in range(dff // FF_CHUNK):
        lo = c * FF_CHUNK
        g = _dot(xb, wup_ref[:, lo:lo + FF_CHUNK])
        u = _dot(xb, wup_ref[:, dff + lo:dff + lo + FF_CHUNK])
        h_scr[:, lo:lo + FF_CHUNK] = (g * (1.0 / (1.0 + jnp.exp(-g))) * u).astype(BF16)


def _ffn_body(x_ref, g_ref, wup_ref, wdn_ref, o_ref, h_scr, *, dff):
    x = x_ref[...]
    _swiglu_into(_rms(x, g_ref[...]).astype(BF16), wup_ref, h_scr, dff)
    o_ref[...] = x + 0.5 * _dot(h_scr[...], wdn_ref[...])


def _ffn(x, g, wup, wdn, tm):
    n, d = x.shape
    dff = wdn.shape[0]
    row = pl.BlockSpec((tm, d), lambda i: (i, 0))
    return pl.pallas_call(
        functools.partial(_ffn_body, dff=dff),
        grid=(n // tm,),
        in_specs=[row, _const_spec((1, d)), _const_spec(wup.shape), _const_spec(wdn.shape)],
        out_specs=row,
        out_shape=jax.ShapeDtypeStruct((n, d), F32),
        scratch_shapes=[pltpu.VMEM((tm, dff), BF16)],
        compiler_params=_params(("parallel",)),
        name="ffn",
    )(x, g, wup, wdn)


def _swap_halves(x):
    lane = lax.broadcasted_iota(jnp.int32, (x.shape[0], LANES), 1)
    first = (lane % HEAD_DIM) < (HEAD_DIM // 2)
    out = []
    for c in range(x.shape[1] // LANES):
        xs = x[:, c * LANES:(c + 1) * LANES]
        up = pltpu.roll(xs, LANES - HEAD_DIM // 2, axis=1)
        down = pltpu.roll(xs, HEAD_DIM // 2, axis=1)
        out.append(jnp.where(first, up, down))
    return jnp.concatenate(out, axis=1)


def _qk_norm_rope(x, g, cos, sin_signed, group_ones):
    ms = _split_dot(x * x, group_ones) * (1.0 / HEAD_DIM)
    y = x * lax.rsqrt(ms + RMS_EPS) * g
    return y * cos + _swap_halves(y) * sin_signed


def _proj_body(x_ref, g_ref, w_ref, gq_ref, gk_ref, cos_ref, sin_ref, ones_ref,
               ka_o, va_o, kb_o, vb_o, qa_b, ka_b, va_b, qb_b, kb_b, vb_b, *, width, scale, feature_major):
    h = _rms(x_ref[...], g_ref[...]).astype(BF16)

    def seg(i):
        return _dot(h, w_ref[:, i * width:(i + 1) * width])

    def stored(x):
        return x.T if feature_major else x

    cos, sin = cos_ref[...], sin_ref[...]
    qa = _qk_norm_rope(seg(0), gq_ref[...], cos, sin, ones_ref[...])
    qa_b[...] = (qa * scale).astype(qa_b.dtype)
    ka = _qk_norm_rope(seg(1), gk_ref[...], cos, sin, ones_ref[...])
    ka_o[...] = stored(ka)
    ka_b[...] = ka.astype(BF16)
    va = seg(2)
    if feature_major:
        n_heads = width // LANES
        for hd in range(n_heads):
            va_o[pl.ds(hd, va.shape[0], stride=n_heads), :] = va[:, hd * LANES:(hd + 1) * LANES]
    else:
        va_o[...] = va
    va_b[...] = va.astype(BF16)
    qb_b[...] = (seg(3) * scale).astype(qb_b.dtype)
    kb = seg(4)
    kb_o[...] = stored(kb)
    kb_b[...] = kb.astype(BF16)
    vb = seg(5)
    vb_o[...] = stored(vb)
    vb_b[...] = vb.astype(BF16)


def _proj(x, g, w_qkv, gq, gk, cos, sin, ones, tm, q_dtype, q_scale, seq=None):
    n, d = x.shape
    width = w_qkv.shape[1] // 6
    n_pos = cos.shape[0]
    row = pl.BlockSpec((tm, d), lambda i: (i, 0))
    out = pl.BlockSpec((tm, width), lambda i: (i, 0))
    if n_pos == 1:
        pos = _const_spec((1, width))
    else:
        pos = pl.BlockSpec((tm, width), lambda i: (i % (n_pos // tm), 0))
    f32o = jax.ShapeDtypeStruct((n, width), F32)
    if seq is None:
        fm_spec, fm_o, va_spec, va_o = out, f32o, out, f32o
    else:
        per_seq = seq // tm
        fm_spec = pl.BlockSpec((None, width, tm), lambda i: (i // per_seq, 0, i % per_seq))
        fm_o = jax.ShapeDtypeStruct((n // seq, width, seq), F32)
        n_heads = width // LANES
        va_spec = pl.BlockSpec((tm * n_heads, LANES), lambda i: (i, 0))
        va_o = jax.ShapeDtypeStruct((n * n_heads, LANES), F32)
    b16o = jax.ShapeDtypeStruct((n, width), BF16)
    qo = jax.ShapeDtypeStruct((n, width), q_dtype)
    return pl.pallas_call(
        functools.partial(_proj_body, width=width, scale=q_scale, feature_major=seq is not None),
        grid=(n // tm,),
        in_specs=[row, _const_spec((1, d)), _const_spec(w_qkv.shape), _const_spec((1, width)),
                  _const_spec((1, width)), pos, pos, _const_spec(ones.shape)],
        out_specs=[fm_spec, va_spec, fm_spec, fm_spec] + [out] * 6,
        out_shape=[fm_o, va_o, fm_o, fm_o, qo, b16o, b16o, qo, b16o, b16o],
        compiler_params=_params(("parallel",)),
        name="proj",
    )(x, g, w_qkv, gq, gk, cos, sin, ones)


def _lam(lam_ref, lam_init):
    p = lam_ref[...]
    s1 = jnp.sum(p[0:1] * p[1:2], axis=-1, keepdims=True)
    s2 = jnp.sum(p[2:3] * p[3:4], axis=-1, keepdims=True)
    return jnp.exp(s1) - jnp.exp(s2) + lam_init


def _attn_a_body(q_ref, k_ref, v_ref, lam_ref, gs_ref, o_ref, *, blk, lam_init):
    i = pl.program_id(2)
    q = q_ref[...]
    lane = lax.broadcasted_iota(jnp.int32, q.shape, 1)
    qs = (jnp.where(lane < HEAD_DIM, q, 0), jnp.where(lane >= HEAD_DIM, q, 0))
    rows = lax.broadcasted_iota(jnp.int32, (blk, blk), 0)
    cols = lax.broadcasted_iota(jnp.int32, (blk, blk), 1)

    def step(j, carry, masked):
        off = pl.multiple_of(j * blk, blk)
        k = k_ref[pl.ds(off, blk), :]
        v = v_ref[pl.ds(off, blk), :]
        new = []
        for mp in range(2):
            m, l, acc = carry[3 * mp:3 * mp + 3]
            s = _dot_nt(qs[mp], k)
            if masked:
                s = jnp.where(cols <= rows, s, -jnp.inf)
            m_new = jnp.maximum(m, jnp.max(s, axis=-1, keepdims=True))
            alpha = jnp.exp(m - m_new)
            p = jnp.exp(s - m_new)
            l = alpha * l + jnp.sum(p, axis=-1, keepdims=True)
            acc = alpha * acc + _dot(p.astype(BF16), v)
            new += [m_new, l, acc]
        return tuple(new)

    init = (jnp.full((blk, 1), -jnp.inf, F32), jnp.zeros((blk, 1), F32), jnp.zeros((blk, LANES), F32)) * 2
    carry = lax.fori_loop(0, i, functools.partial(step, masked=False), init)
    _, l1, a1, _, l2, a2 = step(i, carry, True)
    o = a1 / l1 - _lam(lam_ref, lam_init) * (a2 / l2)
    o_ref[...] = (_rms(o, gs_ref[...]) * (1.0 - lam_init)).astype(o_ref.dtype)


def _attn_b_body(q_ref, k_ref, v_ref, tri_ref, o_ref, *, blk):
    i = pl.program_id(2)
    q = q_ref[...]
    lane = lax.broadcasted_iota(jnp.int32, q.shape, 1)
    qs = (jnp.where(lane < HEAD_DIM, q, 0), jnp.where(lane >= HEAD_DIM, q, 0))
    rows = lax.broadcasted_iota(jnp.int32, (blk, blk), 0)
    cols = lax.broadcasted_iota(jnp.int32, (blk, blk), 1)
    tri = tri_ref[...]

    def step(j, carry, masked):
        off = pl.multiple_of(j * blk, blk)
        k = k_ref[pl.ds(off, blk), :]
        v = v_ref[pl.ds(off, blk), :]
        new = []
        for hd in range(2):
            c, acc = carry[2 * hd:2 * hd + 2]
            z = _dot_nt(qs[hd], k)
            sp = jnp.maximum(z, 0.0) + jnp.log(1.0 + jnp.exp(-jnp.abs(z)))
            if masked:
                sp = jnp.where(cols < rows, sp, 0.0)
            parts = [None] * (blk // SUFFIX_BLOCK)
            for sb in reversed(range(blk // SUFFIX_BLOCK)):
                sl = slice(sb * SUFFIX_BLOCK, (sb + 1) * SUFFIX_BLOCK)
                within = _dot(sp[:, sl].astype(BF16), tri)
                parts[sb] = jnp.exp(z[:, sl] + within + c)
                c = c - jnp.sum(sp[:, sl], axis=-1, keepdims=True)
            a = jnp.concatenate(parts, axis=1)
            if masked:
                a = jnp.where(cols < rows, a, 0.0)
            acc = acc + _dot(a.astype(BF16), v)
            new += [c, acc]
        return tuple(new)

    init = (jnp.zeros((blk, 1), F32), jnp.zeros((blk, LANES), F32)) * 2
    carry = step(i, init, True)
    carry = lax.fori_loop(0, i, lambda jj, cr: step(i - 1 - jj, cr, False), carry)
    o_ref[...] = jnp.where(lane < HEAD_DIM, carry[1], carry[3]).astype(o_ref.dtype)


def _prompt_attention(body, q, k, v, extra, extra_specs, batch, seq, name):
    n, width = q.shape
    blk = ATTN_BLOCK
    nq = seq // blk
    qspec = pl.BlockSpec((blk, LANES), lambda b, h, i: (b * nq + i, h))
    kvspec = pl.BlockSpec((seq, LANES), lambda b, h, i: (b, h))
    return pl.pallas_call(
        body,
        grid=(batch, width // LANES, nq),
        in_specs=[qspec, kvspec, kvspec] + extra_specs,
        out_specs=qspec,
        out_shape=jax.ShapeDtypeStruct((n, width), BF16),
        compiler_params=_params(("parallel", "parallel", "arbitrary")),
        name=name,
    )(q, k, v, *extra)


def _group_rows(x_row, n_rows):
    w = x_row.shape[1]
    r = lax.broadcasted_iota(jnp.int32, (n_rows, w), 0)
    c = lax.broadcasted_iota(jnp.int32, (n_rows, w), 1)
    return jnp.where(c // HEAD_DIM == r, jnp.broadcast_to(x_row, (n_rows, w)), 0.0)


def _decode_body(pt_ref, qa_ref, ka_ref, va_ref, qb_ref, lam_ref, gs_ref, tri_ref, *rest,
                 n_pages, page, lam_init):
    del pt_ref
    pages = rest[:4 * n_pages]
    oa_ref, ob_ref, m_scr, l_scr, acca_scr, c_scr, accb_scr = rest[4 * n_pages:]
    g = pl.program_id(1)
    rows = 8
    scale = HEAD_DIM ** -0.5
    qa_rows = _group_rows(qa_ref[...], rows)
    qa_bf = (qa_rows * scale).astype(BF16)
    qb_bf = (_group_rows(qb_ref[...], rows) * scale).astype(BF16)

    @pl.when(g == 0)
    def _():
        s_self = jnp.sum(qa_rows * ka_ref[...], axis=-1, keepdims=True) * scale
        m_scr[...] = jnp.broadcast_to(s_self, m_scr.shape)
        l_scr[...] = jnp.ones(l_scr.shape, F32)
        acca_scr[...] = jnp.broadcast_to(va_ref[...], acca_scr.shape)
        c_scr[...] = jnp.zeros(c_scr.shape, F32)
        accb_scr[...] = jnp.zeros(accb_scr.shape, F32)

    ka_pages, va_pages, kb_pages, vb_pages = (pages[t * n_pages:(t + 1) * n_pages] for t in range(4))
    n_heads = acca_scr.shape[1] // LANES

    s = jnp.concatenate([_dot(qa_bf, kp[...].astype(BF16)) for kp in ka_pages], axis=1)
    m = m_scr[...]
    m_new = jnp.maximum(m, jnp.max(s, axis=-1, keepdims=True))
    alpha = jnp.exp(m - m_new)[:, 0:1]
    pr = jnp.exp(s - m_new[:, 0:1])
    l_scr[...] = alpha * l_scr[...] + jnp.sum(pr, axis=-1, keepdims=True)
    m_scr[...] = m_new
    pr = pr.astype(BF16)
    acc_a = alpha * acca_scr[...]
    for p, vp in enumerate(va_pages):
        va = jnp.concatenate([vp[pl.ds(h, page, stride=n_heads), :] for h in range(n_heads)], axis=1)
        acc_a = acc_a + _dot(pr[:, p * page:(p + 1) * page], va.astype(BF16))
    acca_scr[...] = acc_a

    z = jnp.concatenate([_dot(qb_bf, kp[...].astype(BF16)) for kp in kb_pages], axis=1)
    log_beta = _log_sigmoid(z)
    log_1mb = log_beta - z
    hi = log_1mb.astype(BF16).astype(F32)
    hi_lo = jnp.concatenate([hi, log_1mb - hi], axis=0).astype(BF16)
    stacked = jnp.concatenate([hi_lo[:, p * page:(p + 1) * page] for p in range(n_pages)], axis=0)
    both = _dot(stacked, tri_ref[...])
    c = c_scr[...]
    weights = []
    for p in range(n_pages):
        bp = both[2 * rows * p:2 * rows * p + rows] + both[2 * rows * p + rows:2 * rows * (p + 1)]
        weights.append(jnp.exp(log_beta[:, p * page:(p + 1) * page] + bp[:, :page] + c).astype(BF16))
        c = c + bp[:, page:]
    c_scr[...] = c
    acc_b = accb_scr[...]
    for a, vp in zip(weights, vb_pages):
        acc_b = acc_b + _dot_nt(a, vp[...].astype(BF16))
    accb_scr[...] = acc_b

    @pl.when(g == pl.num_programs(1) - 1)
    def _():
        w = acc_a.shape[1]
        r = lax.broadcasted_iota(jnp.int32, (rows, w), 0)
        col = lax.broadcasted_iota(jnp.int32, (rows, w), 1)
        an = acc_a / l_scr[:, 0:1]
        own_head = col // LANES == r // 2
        o1 = jnp.sum(jnp.where(own_head & (r % 2 == 0), an, 0.0), axis=0, keepdims=True)
        o2 = jnp.sum(jnp.where(own_head & (r % 2 == 1), an, 0.0), axis=0, keepdims=True)
        o = o1 - _lam(lam_ref, lam_init) * o2
        heads = [_rms(o[:, h * LANES:(h + 1) * LANES], gs_ref[...]) for h in range(w // LANES)]
        oa_ref[...] = jnp.concatenate(heads, axis=1) * (1.0 - lam_init)
        ob_ref[...] = jnp.sum(jnp.where(col // HEAD_DIM == r, acc_b, 0.0), axis=0, keepdims=True)


def _decode(page_table, qa, ka, va, qb, lam_p, gs, tri_ones, caches, lam_init):
    nb, width = qa.shape
    n_tab = page_table.shape[1]
    page = caches[0].shape[2]
    assert all(cch.shape[1:] == (width, page) for cch in caches) and page == LANES
    npg = PAGES_PER_STEP
    steps = n_tab // npg

    def vec(x):
        return x.reshape(nb, 1, width)

    vspec = pl.BlockSpec((None, 1, width), lambda b, g, pt: (b, 0, 0))

    def page_spec(p):
        return pl.BlockSpec((None, width, page),
                            lambda b, g, pt: (pt[b * n_tab + n_tab - 1 - (g * npg + p)], 0, 0))

    page_specs = [page_spec(p) for _ in range(4) for p in range(npg)]
    page_args = [cch for cch in caches for _ in range(npg)]
    o_sds = jax.ShapeDtypeStruct((nb, 1, width), F32)
    grid_spec = pltpu.PrefetchScalarGridSpec(
        num_scalar_prefetch=1,
        grid=(nb, steps),
        in_specs=[vspec, vspec, vspec, vspec, _const_spec(lam_p.shape), _const_spec(gs.shape),
                  _const_spec(tri_ones.shape)] + page_specs,
        out_specs=[vspec, vspec],
        scratch_shapes=[pltpu.VMEM((8, LANES), F32), pltpu.VMEM((8, LANES), F32), pltpu.VMEM((8, width), F32),
                        pltpu.VMEM((8, LANES), F32), pltpu.VMEM((8, width), F32)],
    )
    oa, ob = pl.pallas_call(
        functools.partial(_decode_body, n_pages=npg, page=page, lam_init=lam_init),
        grid_spec=grid_spec,
        out_shape=[o_sds, o_sds],
        compiler_params=_params(("parallel", "arbitrary")),
        name="decode",
    )(page_table.reshape(-1), vec(qa), vec(ka), vec(va), vec(qb), lam_p, gs, tri_ones, *page_args)
    return oa.reshape(nb, width), ob.reshape(nb, width)


def _merge_body(x_ref, oa_ref, ob_ref, gm_ref, wg_ref, woa_ref, wob_ref, wout_ref,
                g2_ref, wup_ref, wdn_ref, o_ref, h_scr, *, dff):
    x = x_ref[...]
    d = x.shape[1]
    h = _rms(x, gm_ref[...]).astype(BF16)
    gate_a = 1.0 / (1.0 + jnp.exp(-_dot(h, wg_ref[:, :d])))
    gate_b = 1.0 / (1.0 + jnp.exp(-_dot(h, wg_ref[:, d:])))
    merged = (gate_a * _dot(oa_ref[...].astype(BF16), woa_ref[...])
              + gate_b * _dot(ob_ref[...].astype(BF16), wob_ref[...]))
    x = x + _dot(merged.astype(BF16), wout_ref[...])
    _swiglu_into(_rms(x, g2_ref[...]).astype(BF16), wup_ref, h_scr, dff)
    o_ref[...] = x + 0.5 * _dot(h_scr[...], wdn_ref[...])


def _merge(x, oa, ob, gm, wg, woa, wob, wout, g2, wup, wdn, tm):
    n, d = x.shape
    dff = wdn.shape[0]
    row = pl.BlockSpec((tm, d), lambda i: (i, 0))
    mix = pl.BlockSpec((tm, oa.shape[1]), lambda i: (i, 0))
    consts = [_const_spec(a.shape) for a in (gm, wg, woa, wob, wout, g2, wup, wdn)]
    return pl.pallas_call(
        functools.partial(_merge_body, dff=dff),
        grid=(n // tm,),
        in_specs=[row, mix, mix] + consts,
        out_specs=row,
        out_shape=jax.ShapeDtypeStruct((n, d), F32),
        scratch_shapes=[pltpu.VMEM((tm, dff), BF16)],
        compiler_params=_params(("parallel",)),
        name="merge",
    )(x, oa, ob, gm, wg, woa, wob, wout, g2, wup, wdn)


def _rope_tables(pos, n_groups):
    half = HEAD_DIM // 2
    inv = ROPE_THETA ** (-jnp.arange(half, dtype=F32) / half)
    ang = pos.astype(F32)[:, None] * inv[None, :]
    ang = jnp.concatenate([ang, ang], axis=-1)
    sign = jnp.concatenate([-jnp.ones((half,), F32), jnp.ones((half,), F32)])
    return jnp.tile(jnp.cos(ang), (1, n_groups)), jnp.tile(jnp.sin(ang) * sign, (1, n_groups))


def kernel(x_prompt, x_sample, cache_a_k, cache_a_v, cache_b_k, cache_b_v, page_table, g_ffn1, w_ffn1_up, w_ffn1_down, g_mix, w_in, g_q_a, g_k_a, lambda_q1, lambda_k1, lambda_q2, lambda_k2, g_subln_a, w_o_a, w_o_b, w_out, g_ffn2, w_ffn2_up, w_ffn2_down):
    batch, seq, d = x_prompt.shape
    nb, dec_seq, _ = x_sample.shape
    depth, n_phys, page = cache_a_k.shape[:3]
    assert depth == 1 and dec_seq == 1
    width = w_o_a.shape[1]
    n_groups = width // HEAD_DIM
    past_len = page_table.shape[1] * page
    lam_init = _lambda_init(0)

    def row(v):
        return v.reshape(1, -1)

    w_qkv = w_in[0, :, :6 * width].astype(BF16)
    w_gate = w_in[0, :, 6 * width:].astype(BF16)
    wup1, wdn1 = w_ffn1_up[0].astype(BF16), w_ffn1_down[0].astype(BF16)
    wup2, wdn2 = w_ffn2_up[0].astype(BF16), w_ffn2_down[0].astype(BF16)
    woa, wob, wout = w_o_a[0].astype(BF16), w_o_b[0].astype(BF16), w_out[0].astype(BF16)
    gq = jnp.tile(row(g_q_a[0]), (1, n_groups))
    gk = jnp.tile(row(g_k_a[0]), (1, n_groups))
    gs = row(g_subln_a[0])
    lam_p = jnp.stack([lambda_q1[0], lambda_k1[0], lambda_q2[0], lambda_k2[0]])
    idx = jnp.arange(width, dtype=jnp.int32) // HEAD_DIM
    group_ones = (idx[:, None] == idx[None, :]).astype(BF16)
    blk_idx = jnp.arange(SUFFIX_BLOCK, dtype=jnp.int32)
    tri = -(blk_idx[:, None] >= blk_idx[None, :]).astype(BF16)

    def trunk_front(x, pos, tm, q_dtype, q_scale, seq=None):
        x1 = _ffn(x, row(g_ffn1[0]), wup1, wdn1, tm)
        cos, sin = _rope_tables(pos, n_groups)
        return x1, _proj(x1, row(g_mix[0]), w_qkv, gq, gk, cos, sin, group_ones, tm, q_dtype, q_scale, seq)

    def trunk_back(x1, oa, ob, tm):
        return _merge(x1, oa, ob, row(g_mix[0]), w_gate, woa, wob, wout, row(g_ffn2[0]), wup2, wdn2, tm)

    n = batch * seq
    x1p, (ka_p, va_p, kb_p, vb_p, qa_h, ka_h, va_h, qb_h, kb_h, vb_h) = trunk_front(
        x_prompt.reshape(n, d), jnp.arange(seq, dtype=jnp.int32), 256, BF16, HEAD_DIM ** -0.5, seq)
    oa_p = _prompt_attention(
        functools.partial(_attn_a_body, blk=ATTN_BLOCK, lam_init=lam_init), qa_h, ka_h, va_h,
        (lam_p, gs), [_const_spec(lam_p.shape), _const_spec(gs.shape)], batch, seq, "attn_a")
    ob_p = _prompt_attention(
        functools.partial(_attn_b_body, blk=ATTN_BLOCK), qb_h, kb_h, vb_h,
        (tri,), [_const_spec(tri.shape)], batch, seq, "attn_b")
    y_prompt = trunk_back(x1p, oa_p, ob_p, 256).reshape(batch, seq, d)

    x1s, (ka_s, va_s, kb_s, vb_s, qa_s, _, _, qb_s, _, _) = trunk_front(
        x_sample.reshape(nb, d), past_len + jnp.arange(dec_seq, dtype=jnp.int32), nb, F32, 1.0)
    caches = [jnp.transpose(cache_a_k[0], (0, 2, 3, 4, 1)).reshape(n_phys, width, page),
              cache_a_v[0].reshape(n_phys, page * (width // LANES), LANES),
              jnp.transpose(cache_b_k[0], (0, 2, 3, 1)).reshape(n_phys, width, page),
              jnp.transpose(cache_b_v[0], (0, 2, 3, 1)).reshape(n_phys, width, page)]
    pg_idx = jnp.arange(page, dtype=jnp.int32)
    tri_ones = jnp.concatenate([(pg_idx[:, None] > pg_idx[None, :]).astype(BF16), jnp.ones((page, page), BF16)], axis=1)
    oa_s, ob_s = _decode(page_table, qa_s, ka_s, va_s, qb_s, lam_p, gs, tri_ones, caches, lam_init)
    y_sample = trunk_back(x1s, oa_s, ob_s, nb).reshape(nb, dec_seq, d)

    n_a, n_b = w_o_a.shape[1] // (2 * HEAD_DIM), w_o_b.shape[1] // HEAD_DIM
    return (y_prompt, y_sample,
            jnp.transpose(ka_p.reshape(1, batch, n_a, 2, HEAD_DIM, seq), (0, 1, 5, 2, 3, 4)),
            va_p.reshape(1, batch, seq, n_a, 2 * HEAD_DIM),
            jnp.transpose(kb_p.reshape(1, batch, n_b, HEAD_DIM, seq), (0, 1, 4, 2, 3)),
            jnp.transpose(vb_p.reshape(1, batch, n_b, HEAD_DIM, seq), (0, 1, 4, 2, 3)),
            ka_s.reshape(1, nb, dec_seq, n_a, 2, HEAD_DIM), va_s.reshape(1, nb, dec_seq, n_a, 2 * HEAD_DIM),
            kb_s.reshape(1, nb, dec_seq, n_b, HEAD_DIM), vb_s.reshape(1, nb, dec_seq, n_b, HEAD_DIM))
```

```python
import functools
import math

import jax
import jax.numpy as jnp
from jax import lax
from jax.experimental import pallas as pl
from jax.experimental.pallas import tpu as pltpu

F32 = jnp.float32
BF16 = jnp.bfloat16

RMS_EPS = 1e-6
ROPE_THETA = 10000.0
HEAD_DIM = 64
LANES = 128
VMEM_LIMIT_BYTES = 56 * 1024 * 1024
FF_CHUNK = 256
ATTN_BLOCK = 512
SUFFIX_BLOCK = 256
PAGES_PER_STEP = 16


def _lambda_init(layer_idx):
    return 0.8 - 0.6 * math.exp(-0.3 * layer_idx)


def _const_spec(shape):
    nd = len(shape)
    return pl.BlockSpec(shape, lambda *_: (0,) * nd, pipeline_mode=pl.Buffered(1))


def _params(sem):
    return pltpu.CompilerParams(dimension_semantics=sem, vmem_limit_bytes=VMEM_LIMIT_BYTES)


def _rms(x, g):
    return x * lax.rsqrt(jnp.mean(x * x, axis=-1, keepdims=True) + RMS_EPS) * g


def _dot(a, b):
    return jnp.dot(a, b, preferred_element_type=F32)


def _dot_nt(a, b):
    return lax.dot_general(a, b, (((1,), (1,)), ((), ())), preferred_element_type=F32)


def _split_dot(x, w):
    hi = x.astype(BF16)
    lo = (x - hi.astype(F32)).astype(BF16)
    return _dot(hi, w) + _dot(lo, w)


def _log_sigmoid(z):
    return jnp.minimum(z, 0.0) - jnp.log(1.0 + jnp.exp(-jnp.abs(z)))


def _swiglu_into(xb, wup_ref, h_scr, dff):
    for c in range(dff // FF_CHUNK):
        lo = c * FF_CHUNK
        g = _dot(xb, wup_ref[:, lo:lo + FF_CHUNK])
        u = _dot(xb, wup_ref[:, dff + lo:dff + lo + FF_CHUNK])
        h_scr[:, lo:lo + FF_CHUNK] = (g * (1.0 / (1.0 + jnp.exp(-g))) * u).astype(BF16)


def _ffn_body(x_ref, g_ref, wup_ref, wdn_ref, o_ref, h_scr, *, dff):
    x = x_ref[...]
    _swiglu_into(_rms(x, g_ref[...]).astype(BF16), wup_ref, h_scr, dff)
    o_ref[...] = x + 0.5 * _dot(h_scr[...], wdn_ref[...])


def _ffn(x, g, wup, wdn, tm):
    n, d = x.shape
    dff = wdn.shape[0]
    row = pl.BlockSpec((tm, d), lambda i: (i, 0))
    return pl.pallas_call(
        functools.partial(_ffn_body, dff=dff),
        grid=(n // tm,),
        in_specs=[row, _const_spec((1, d)), _const_spec(wup.shape), _const_spec(wdn.shape)],
        out_specs=row,
        out_shape=jax.ShapeDtypeStruct((n, d), F32),
        scratch_shapes=[pltpu.VMEM((tm, dff), BF16)],
        compiler_params=_params(("parallel",)),
        name="ffn",
    )(x, g, wup, wdn)


def _swap_halves(x):
    lane = lax.broadcasted_iota(jnp.int32, (x.shape[0], LANES), 1)
    first = (lane % HEAD_DIM) < (HEAD_DIM // 2)
    out = []
    for c in range(x.shape[1] // LANES):
        xs = x[:, c * LANES:(c + 1) * LANES]
        up = pltpu.roll(xs, LANES - HEAD_DIM // 2, axis=1)
        down = pltpu.roll(xs, HEAD_DIM // 2, axis=1)
        out.append(jnp.where(first, up, down))
    return jnp.concatenate(out, axis=1)


def _qk_norm_rope(x, g, cos, sin_signed, group_ones):
    ms = _split_dot(x * x, group_ones) * (1.0 / HEAD_DIM)
    y = x * lax.rsqrt(ms + RMS_EPS) * g
    return y * cos + _swap_halves(y) * sin_signed


def _proj_body(x_ref, g_ref, w_ref, gq_ref, gk_ref, cos_ref, sin_ref, ones_ref,
               ka_o, va_o, kb_o, vb_o, qa_b, ka_b, va_b, qb_b, kb_b, vb_b, *, width, scale, feature_major):
    h = _rms(x_ref[...], g_ref[...]).astype(BF16)

    def seg(i):
        return _dot(h, w_ref[:, i * width:(i + 1) * width])

    def stored(x):
        return x.T if feature_major else x

    cos, sin = cos_ref[...], sin_ref[...]
    qa = _qk_norm_rope(seg(0), gq_ref[...], cos, sin, ones_ref[...])
    qa_b[...] = (qa * scale).astype(qa_b.dtype)
    ka = _qk_norm_rope(seg(1), gk_ref[...], cos, sin, ones_ref[...])
    ka_o[...] = stored(ka)
    ka_b[...] = ka.astype(BF16)
    va = seg(2)
    if feature_major:
        n_heads = width // LANES
        for hd in range(n_heads):
            va_o[pl.ds(hd, va.shape[0], stride=n_heads), :] = va[:, hd * LANES:(hd + 1) * LANES]
    else:
        va_o[...] = va
    va_b[...] = va.astype(BF16)
    qb_b[...] = (seg(3) * scale).astype(qb_b.dtype)
    kb = seg(4)
    kb_o[...] = stored(kb)
    kb_b[...] = kb.astype(BF16)
    vb = seg(5)
    vb_o[...] = stored(vb)
    vb_b[...] = vb.astype(BF16)


def _proj(x, g, w_qkv, gq, gk, cos, sin, ones, tm, q_dtype, q_scale, seq=None):
    n, d = x.shape
    width = w_qkv.shape[1] // 6
    n_pos = cos.shape[0]
    row = pl.BlockSpec((tm, d), lambda i: (i, 0))
    out = pl.BlockSpec((tm, width), lambda i: (i, 0))
    if n_pos == 1:
        pos = _const_spec((1, width))
    else:
        pos = pl.BlockSpec((tm, width), lambda i: (i % (n_pos // tm), 0))
    f32o = jax.ShapeDtypeStruct((n, width), F32)
    if seq is None:
        fm_spec, fm_o, va_spec, va_o = out, f32o, out, f32o
    else:
        per_seq = seq // tm
        fm_spec = pl.BlockSpec((None, width, tm), lambda i: (i // per_seq, 0, i % per_seq))
        fm_o = jax.ShapeDtypeStruct((n // seq, width, seq), F32)
        n_heads = width // LANES
        va_spec = pl.BlockSpec((tm * n_heads, LANES), lambda i: (i, 0))
        va_o = jax.ShapeDtypeStruct((n * n_heads, LANES), F32)
    b16o = jax.ShapeDtypeStruct((n, width), BF16)
    qo = jax.ShapeDtypeStruct((n, width), q_dtype)
    return pl.pallas_call(
        functools.partial(_proj_body, width=width, scale=q_scale, feature_major=seq is not None),
        grid=(n // tm,),
        in_specs=[row, _const_spec((1, d)), _const_spec(w_qkv.shape), _const_spec((1, width)),
                  _const_spec((1, width)), pos, pos, _const_spec(ones.shape)],
        out_specs=[fm_spec, va_spec, fm_spec, fm_spec] + [out] * 6,
        out_shape=[fm_o, va_o, fm_o, fm_o, qo, b16o, b16o, qo, b16o, b16o],
        compiler_params=_params(("parallel",)),
        name="proj",
    )(x, g, w_qkv, gq, gk, cos, sin, ones)


def _lam(lam_ref, lam_init):
    p = lam_ref[...]
    s1 = jnp.sum(p[0:1] * p[1:2], axis=-1, keepdims=True)
    s2 = jnp.sum(p[2:3] * p[3:4], axis=-1, keepdims=True)
    return jnp.exp(s1) - jnp.exp(s2) + lam_init


def _attn_a_body(q_ref, k_ref, v_ref, lam_ref, gs_ref, o_ref, *, blk, lam_init):
    i = pl.program_id(2)
    q = q_ref[...]
    lane = lax.broadcasted_iota(jnp.int32, q.shape, 1)
    qs = (jnp.where(lane < HEAD_DIM, q, 0), jnp.where(lane >= HEAD_DIM, q, 0))
    rows = lax.broadcasted_iota(jnp.int32, (blk, blk), 0)
    cols = lax.broadcasted_iota(jnp.int32, (blk, blk), 1)

    def step(j, carry, masked):
        off = pl.multiple_of(j * blk, blk)
        k = k_ref[pl.ds(off, blk), :]
        v = v_ref[pl.ds(off, blk), :]
        new = []
        for mp in range(2):
            m, l, acc = carry[3 * mp:3 * mp + 3]
            s = _dot_nt(qs[mp], k)
            if masked:
                s = jnp.where(cols <= rows, s, -jnp.inf)
            m_new = jnp.maximum(m, jnp.max(s, axis=-1, keepdims=True))
            alpha = jnp.exp(m - m_new)
            p = jnp.exp(s - m_new)
            l = alpha * l + jnp.sum(p, axis=-1, keepdims=True)
            acc = alpha * acc + _dot(p.astype(BF16), v)
            new += [m_new, l, acc]
        return tuple(new)

    init = (jnp.full((blk, 1), -jnp.inf, F32), jnp.zeros((blk, 1), F32), jnp.zeros((blk, LANES), F32)) * 2
    carry = lax.fori_loop(0, i // 2, lambda jj, cr: step(2 * jj + 1, step(2 * jj, cr, False), False), init)
    carry = lax.cond(i % 2 == 1, lambda cr: step(i - 1, cr, False), lambda cr: cr, carry)
    _, l1, a1, _, l2, a2 = step(i, carry, True)
    o = a1 / l1 - _lam(lam_ref, lam_init) * (a2 / l2)
    o_ref[...] = (_rms(o, gs_ref[...]) * (1.0 - lam_init)).astype(o_ref.dtype)


def _attn_b_body(q_ref, k_ref, v_ref, tri_ref, o_ref, *, blk):
    i = pl.program_id(2)
    q = q_ref[...]
    lane = lax.broadcasted_iota(jnp.int32, q.shape, 1)
    qs = (jnp.where(lane < HEAD_DIM, q, 0), jnp.where(lane >= HEAD_DIM, q, 0))
    rows = lax.broadcasted_iota(jnp.int32, (blk, blk), 0)
    cols = lax.broadcasted_iota(jnp.int32, (blk, blk), 1)
    tri = tri_ref[...]

    def step(j, carry, masked):
        off = pl.multiple_of(j * blk, blk)
        k = k_ref[pl.ds(off, blk), :]
        v = v_ref[pl.ds(off, blk), :]
        new = []
        for hd in range(2):
            c, acc = carry[2 * hd:2 * hd + 2]
            z = _dot_nt(qs[hd], k)
            sp = jnp.maximum(z, 0.0) + jnp.log(1.0 + jnp.exp(-jnp.abs(z)))
            if masked:
                sp = jnp.where(cols < rows, sp, 0.0)
            parts = [None] * (blk // SUFFIX_BLOCK)
            for sb in reversed(range(blk // SUFFIX_BLOCK)):
                sl = slice(sb * SUFFIX_BLOCK, (sb + 1) * SUFFIX_BLOCK)
                within = _dot(sp[:, sl].astype(BF16), tri)
                parts[sb] = jnp.exp(z[:, sl] + within + c)
                c = c - jnp.sum(sp[:, sl], axis=-1, keepdims=True)
            a = jnp.concatenate(parts, axis=1)
            if masked:
                a = jnp.where(cols < rows, a, 0.0)
            acc = acc + _dot(a.astype(BF16), v)
            new += [c, acc]
        return tuple(new)

    init = (jnp.zeros((blk, 1), F32), jnp.zeros((blk, LANES), F32)) * 2
    carry = step(i, init, True)
    carry = lax.fori_loop(0, i // 2, lambda jj, cr: step(i - 2 - 2 * jj, step(i - 1 - 2 * jj, cr, False), False),
                          carry)
    carry = lax.cond(i % 2 == 1, lambda cr: step(0, cr, False), lambda cr: cr, carry)
    o_ref[...] = jnp.where(lane < HEAD_DIM, carry[1], carry[3]).astype(o_ref.dtype)


def _prompt_attention(body, q, k, v, extra, extra_specs, batch, seq, name):
    n, width = q.shape
    blk = ATTN_BLOCK
    nq = seq // blk
    qspec = pl.BlockSpec((blk, LANES), lambda b, h, i: (b * nq + i, h))
    kvspec = pl.BlockSpec((seq, LANES), lambda b, h, i: (b, h))
    return pl.pallas_call(
        body,
        grid=(batch, width // LANES, nq),
        in_specs=[qspec, kvspec, kvspec] + extra_specs,
        out_specs=qspec,
        out_shape=jax.ShapeDtypeStruct((n, width), BF16),
        compiler_params=_params(("parallel", "parallel", "arbitrary")),
        name=name,
    )(q, k, v, *extra)


def _group_rows(x_row, n_rows):
    w = x_row.shape[1]
    r = lax.broadcasted_iota(jnp.int32, (n_rows, w), 0)
    c = lax.broadcasted_iota(jnp.int32, (n_rows, w), 1)
    return jnp.where(c // HEAD_DIM == r, jnp.broadcast_to(x_row, (n_rows, w)), 0.0)


def _decode_body(pt_ref, qa_ref, ka_ref, va_ref, qb_ref, lam_ref, gs_ref, tri_ref, *rest,
                 n_pages, page, lam_init):
    del pt_ref
    pages = rest[:4 * n_pages]
    oa_ref, ob_ref, m_scr, l_scr, acca_scr, c_scr, accb_scr = rest[4 * n_pages:]
    g = pl.program_id(1)
    rows = 8
    scale = HEAD_DIM ** -0.5
    qa_rows = _group_rows(qa_ref[...], rows)
    qa_bf = (qa_rows * scale).astype(BF16)
    qb_bf = (_group_rows(qb_ref[...], rows) * scale).astype(BF16)

    @pl.when(g == 0)
    def _():
        s_self = jnp.sum(qa_rows * ka_ref[...], axis=-1, keepdims=True) * scale
        m_scr[...] = jnp.broadcast_to(s_self, m_scr.shape)
        l_scr[...] = jnp.ones(l_scr.shape, F32)
        acca_scr[...] = jnp.broadcast_to(va_ref[...], acca_scr.shape)
        c_scr[...] = jnp.zeros(c_scr.shape, F32)
        accb_scr[...] = jnp.zeros(accb_scr.shape, F32)

    ka_pages, va_pages, kb_pages, vb_pages = (pages[t * n_pages:(t + 1) * n_pages] for t in range(4))
    n_heads = acca_scr.shape[1] // LANES

    s = jnp.concatenate([_dot(qa_bf, kp[...].astype(BF16)) for kp in ka_pages], axis=1)
    m = m_scr[...]
    m_new = jnp.maximum(m, jnp.max(s, axis=-1, keepdims=True))
    alpha = jnp.exp(m - m_new)[:, 0:1]
    pr = jnp.exp(s - m_new[:, 0:1])
    l_scr[...] = alpha * l_scr[...] + jnp.sum(pr, axis=-1, keepdims=True)
    m_scr[...] = m_new
    pr = pr.astype(BF16)
    acc_a = alpha * acca_scr[...]
    for p, vp in enumerate(va_pages):
        va = jnp.concatenate([vp[pl.ds(h, page, stride=n_heads), :] for h in range(n_heads)], axis=1)
        acc_a = acc_a + _dot(pr[:, p * page:(p + 1) * page], va.astype(BF16))
    acca_scr[...] = acc_a

    z = jnp.concatenate([_dot(qb_bf, kp[...].astype(BF16)) for kp in kb_pages], axis=1)
    log_beta = _log_sigmoid(z)
    log_1mb = log_beta - z
    hi = log_1mb.astype(BF16).astype(F32)
    hi_lo = jnp.concatenate([hi, log_1mb - hi], axis=0).astype(BF16)
    stacked = jnp.concatenate([hi_lo[:, p * page:(p + 1) * page] for p in range(n_pages)], axis=0)
    both = _dot(stacked, tri_ref[...])
    c = c_scr[...]
    weights = []
    for p in range(n_pages):
        bp = both[2 * rows * p:2 * rows * p + rows] + both[2 * rows * p + rows:2 * rows * (p + 1)]
        weights.append(jnp.exp(log_beta[:, p * page:(p + 1) * page] + bp[:, :page] + c).astype(BF16))
        c = c + bp[:, page:]
    c_scr[...] = c
    acc_b = accb_scr[...]
    for a, vp in zip(weights, vb_pages):
        acc_b = acc_b + _dot_nt(a, vp[...].astype(BF16))
    accb_scr[...] = acc_b

    @pl.when(g == pl.num_programs(1) - 1)
    def _():
        w = acc_a.shape[1]
        r = lax.broadcasted_iota(jnp.int32, (rows, w), 0)
        col = lax.broadcasted_iota(jnp.int32, (rows, w), 1)
        an = acc_a / l_scr[:, 0:1]
        own_head = col // LANES == r // 2
        o1 = jnp.sum(jnp.where(own_head & (r % 2 == 0), an, 0.0), axis=0, keepdims=True)
        o2 = jnp.sum(jnp.where(own_head & (r % 2 == 1), an, 0.0), axis=0, keepdims=True)
        o = o1 - _lam(lam_ref, lam_init) * o2
        heads = [_rms(o[:, h * LANES:(h + 1) * LANES], gs_ref[...]) for h in range(w // LANES)]
        oa_ref[...] = jnp.concatenate(heads, axis=1) * (1.0 - lam_init)
        ob_ref[...] = jnp.sum(jnp.where(col // HEAD_DIM == r, acc_b, 0.0), axis=0, keepdims=True)


def _decode(page_table, qa, ka, va, qb, lam_p, gs, tri_ones, caches, lam_init):
    nb, width = qa.shape
    n_tab = page_table.shape[1]
    page = caches[0].shape[2]
    assert all(cch.shape[1:] == (width, page) for cch in caches) and page == LANES
    npg = PAGES_PER_STEP
    steps = n_tab // npg

    def vec(x):
        return x.reshape(nb, 1, width)

    vspec = pl.BlockSpec((None, 1, width), lambda b, g, pt: (b, 0, 0))

    def page_spec(p):
        return pl.BlockSpec((None, width, page),
                            lambda b, g, pt: (pt[b * n_tab + n_tab - 1 - (g * npg + p)], 0, 0))

    page_specs = [page_spec(p) for _ in range(4) for p in range(npg)]
    page_args = [cch for cch in caches for _ in range(npg)]
    o_sds = jax.ShapeDtypeStruct((nb, 1, width), F32)
    grid_spec = pltpu.PrefetchScalarGridSpec(
        num_scalar_prefetch=1,
        grid=(nb, steps),
        in_specs=[vspec, vspec, vspec, vspec, _const_spec(lam_p.shape), _const_spec(gs.shape),
                  _const_spec(tri_ones.shape)] + page_specs,
        out_specs=[vspec, vspec],
        scratch_shapes=[pltpu.VMEM((8, LANES), F32), pltpu.VMEM((8, LANES), F32), pltpu.VMEM((8, width), F32),
                        pltpu.VMEM((8, LANES), F32), pltpu.VMEM((8, width), F32)],
    )
    oa, ob = pl.pallas_call(
        functools.partial(_decode_body, n_pages=npg, page=page, lam_init=lam_init),
        grid_spec=grid_spec,
        out_shape=[o_sds, o_sds],
        compiler_params=_params(("parallel", "arbitrary")),
        name="decode",
    )(page_table.reshape(-1), vec(qa), vec(ka), vec(va), vec(qb), lam_p, gs, tri_ones, *page_args)
    return oa.reshape(nb, width), ob.reshape(nb, width)


def _merge_body(x_ref, oa_ref, ob_ref, gm_ref, wg_ref, woa_ref, wob_ref, wout_ref,
                g2_ref, wup_ref, wdn_ref, o_ref, h_scr, *, dff):
    x = x_ref[...]
    d = x.shape[1]
    h = _rms(x, gm_ref[...]).astype(BF16)
    gate_a = 1.0 / (1.0 + jnp.exp(-_dot(h, wg_ref[:, :d])))
    gate_b = 1.0 / (1.0 + jnp.exp(-_dot(h, wg_ref[:, d:])))
    merged = (gate_a * _dot(oa_ref[...].astype(BF16), woa_ref[...])
              + gate_b * _dot(ob_ref[...].astype(BF16), wob_ref[...]))
    x = x + _dot(merged.astype(BF16), wout_ref[...])
    _swiglu_into(_rms(x, g2_ref[...]).astype(BF16), wup_ref, h_scr, dff)
    o_ref[...] = x + 0.5 * _dot(h_scr[...], wdn_ref[...])


def _merge(x, oa, ob, gm, wg, woa, wob, wout, g2, wup, wdn, tm):
    n, d = x.shape
    dff = wdn.shape[0]
    row = pl.BlockSpec((tm, d), lambda i: (i, 0))
    mix = pl.BlockSpec((tm, oa.shape[1]), lambda i: (i, 0))
    consts = [_const_spec(a.shape) for a in (gm, wg, woa, wob, wout, g2, wup, wdn)]
    return pl.pallas_call(
        functools.partial(_merge_body, dff=dff),
        grid=(n // tm,),
        in_specs=[row, mix, mix] + consts,
        out_specs=row,
        out_shape=jax.ShapeDtypeStruct((n, d), F32),
        scratch_shapes=[pltpu.VMEM((tm, dff), BF16)],
        compiler_params=_params(("parallel",)),
        name="merge",
    )(x, oa, ob, gm, wg, woa, wob, wout, g2, wup, wdn)


def _rope_tables(pos, n_groups):
    half = HEAD_DIM // 2
    inv = ROPE_THETA ** (-jnp.arange(half, dtype=F32) / half)
    ang = pos.astype(F32)[:, None] * inv[None, :]
    ang = jnp.concatenate([ang, ang], axis=-1)
    sign = jnp.concatenate([-jnp.ones((half,), F32), jnp.ones((half,), F32)])
    return jnp.tile(jnp.cos(ang), (1, n_groups)), jnp.tile(jnp.sin(ang) * sign, (1, n_groups))


def kernel(x_prompt, x_sample, cache_a_k, cache_a_v, cache_b_k, cache_b_v, page_table, g_ffn1, w_ffn1_up, w_ffn1_down, g_mix, w_in, g_q_a, g_k_a, lambda_q1, lambda_k1, lambda_q2, lambda_k2, g_subln_a, w_o_a, w_o_b, w_out, g_ffn2, w_ffn2_up, w_ffn2_down):
    batch, seq, d = x_prompt.shape
    nb, dec_seq, _ = x_sample.shape
    depth, n_phys, page = cache_a_k.shape[:3]
    assert depth == 1 and dec_seq == 1
    width = w_o_a.shape[1]
    n_groups = width // HEAD_DIM
    past_len = page_table.shape[1] * page
    lam_init = _lambda_init(0)

    def row(v):
        return v.reshape(1, -1)

    w_qkv = w_in[0, :, :6 * width].astype(BF16)
    w_gate = w_in[0, :, 6 * width:].astype(BF16)
    wup1, wdn1 = w_ffn1_up[0].astype(BF16), w_ffn1_down[0].astype(BF16)
    wup2, wdn2 = w_ffn2_up[0].astype(BF16), w_ffn2_down[0].astype(BF16)
    woa, wob, wout = w_o_a[0].astype(BF16), w_o_b[0].astype(BF16), w_out[0].astype(BF16)
    gq = jnp.tile(row(g_q_a[0]), (1, n_groups))
    gk = jnp.tile(row(g_k_a[0]), (1, n_groups))
    gs = row(g_subln_a[0])
    lam_p = jnp.stack([lambda_q1[0], lambda_k1[0], lambda_q2[0], lambda_k2[0]])
    idx = jnp.arange(width, dtype=jnp.int32) // HEAD_DIM
    group_ones = (idx[:, None] == idx[None, :]).astype(BF16)
    blk_idx = jnp.arange(SUFFIX_BLOCK, dtype=jnp.int32)
    tri = -(blk_idx[:, None] >= blk_idx[None, :]).astype(BF16)

    def trunk_front(x, pos, tm, q_dtype, q_scale, seq=None):
        x1 = _ffn(x, row(g_ffn1[0]), wup1, wdn1, tm)
        cos, sin = _rope_tables(pos, n_groups)
        return x1, _proj(x1, row(g_mix[0]), w_qkv, gq, gk, cos, sin, group_ones, tm, q_dtype, q_scale, seq)

    def trunk_back(x1, oa, ob, tm):
        return _merge(x1, oa, ob, row(g_mix[0]), w_gate, woa, wob, wout, row(g_ffn2[0]), wup2, wdn2, tm)

    n = batch * seq
    x1p, (ka_p, va_p, kb_p, vb_p, qa_h, ka_h, va_h, qb_h, kb_h, vb_h) = trunk_front(
        x_prompt.reshape(n, d), jnp.arange(seq, dtype=jnp.int32), 256, BF16, HEAD_DIM ** -0.5, seq)
    oa_p = _prompt_attention(
        functools.partial(_attn_a_body, blk=ATTN_BLOCK, lam_init=lam_init), qa_h, ka_h, va_h,
        (lam_p, gs), [_const_spec(lam_p.shape), _const_spec(gs.shape)], batch, seq, "attn_a")
    ob_p = _prompt_attention(
        functools.partial(_attn_b_body, blk=ATTN_BLOCK), qb_h, kb_h, vb_h,
        (tri,), [_const_spec(tri.shape)], batch, seq, "attn_b")
    y_prompt = trunk_back(x1p, oa_p, ob_p, 256).reshape(batch, seq, d)

    x1s, (ka_s, va_s, kb_s, vb_s, qa_s, _, _, qb_s, _, _) = trunk_front(
        x_sample.reshape(nb, d), past_len + jnp.arange(dec_seq, dtype=jnp.int32), nb, F32, 1.0)
    caches = [jnp.transpose(cache_a_k[0], (0, 2, 3, 4, 1)).reshape(n_phys, width, page),
              cache_a_v[0].reshape(n_phys, page * (width // LANES), LANES),
              jnp.transpose(cache_b_k[0], (0, 2, 3, 1)).reshape(n_phys, width, page),
              jnp.transpose(cache_b_v[0], (0, 2, 3, 1)).reshape(n_phys, width, page)]
    pg_idx = jnp.arange(page, dtype=jnp.int32)
    tri_ones = jnp.concatenate([(pg_idx[:, None] > pg_idx[None, :]).astype(BF16), jnp.ones((page, page), BF16)], axis=1)
    oa_s, ob_s = _decode(page_table, qa_s, ka_s, va_s, qb_s, lam_p, gs, tri_ones, caches, lam_init)
    y_sample = trunk_back(x1s, oa_s, ob_s, nb).reshape(nb, dec_seq, d)

    n_a, n_b = w_o_a.shape[1] // (2 * HEAD_DIM), w_o_b.shape[1] // HEAD_DIM
    return (y_prompt, y_sample,
            jnp.transpose(ka_p.reshape(1, batch, n_a, 2, HEAD_DIM, seq), (0, 1, 5, 2, 3, 4)),
            va_p.reshape(1, batch, seq, n_a, 2 * HEAD_DIM),
            jnp.transpose(kb_p.reshape(1, batch, n_b, HEAD_DIM, seq), (0, 1, 4, 2, 3)),
            jnp.transpose(vb_p.reshape(1, batch, n_b, HEAD_DIM, seq), (0, 1, 4, 2, 3)),
            ka_s.reshape(1, nb, dec_seq, n_a, 2, HEAD_DIM), va_s.reshape(1, nb, dec_seq, n_a, 2 * HEAD_DIM),
            kb_s.reshape(1, nb, dec_seq, n_b, HEAD_DIM), vb_s.reshape(1, nb, dec_seq, n_b, HEAD_DIM))
```

```python
import functools
import math

import jax
import jax.numpy as jnp
from jax import lax
from jax.experimental import pallas as pl
from jax.experimental.pallas import tpu as pltpu

F32 = jnp.float32
BF16 = jnp.bfloat16

RMS_EPS = 1e-6
ROPE_THETA = 10000.0
HEAD_DIM = 64
LANES = 128
VMEM_LIMIT_BYTES = 56 * 1024 * 1024
FF_CHUNK = 256
ATTN_BLOCK = 512
SUFFIX_BLOCK = 256
PAGES_PER_STEP = 16


def _lambda_init(layer_idx):
    return 0.8 - 0.6 * math.exp(-0.3 * layer_idx)


def _const_spec(shape):
    nd = len(shape)
    return pl.BlockSpec(shape, lambda *_: (0,) * nd, pipeline_mode=pl.Buffered(1))


def _params(sem):
    return pltpu.CompilerParams(dimension_semantics=sem, vmem_limit_bytes=VMEM_LIMIT_BYTES)


def _rms(x, g):
    return x * lax.rsqrt(jnp.mean(x * x, axis=-1, keepdims=True) + RMS_EPS) * g


def _dot(a, b):
    return jnp.dot(a, b, preferred_element_type=F32)


def _dot_nt(a, b):
    return lax.dot_general(a, b, (((1,), (1,)), ((), ())), preferred_element_type=F32)


def _split_dot(x, w):
    hi = x.astype(BF16)
    lo = (x - hi.astype(F32)).astype(BF16)
    return _dot(hi, w) + _dot(lo, w)


def _log_sigmoid(z):
    return jnp.minimum(z, 0.0) - jnp.log(1.0 + jnp.exp(-jnp.abs(z)))


def _swiglu_into(xb, wup_ref, h_scr, dff):
    for c in range(dff // FF_CHUNK):
        lo = c * FF_CHUNK
        g = _dot(xb, wup_ref[:, lo:lo + FF_CHUNK])
        u = _dot(xb, wup_ref[:, dff + lo:dff + lo + FF_CHUNK])
        h_scr[:, lo:lo + FF_CHUNK] = (g * (1.0 / (1.0 + jnp.exp(-g))) * u).astype(BF16)


def _ffn_body(x_ref, g_ref, wup_ref, wdn_ref, o_ref, h_scr, *, dff):
    x = x_ref[...]
    _swiglu_into(_rms(x, g_ref[...]).astype(BF16), wup_ref, h_scr, dff)
    o_ref[...] = x + 0.5 * _dot(h_scr[...], wdn_ref[...])


def _ffn(x, g, wup, wdn, tm):
    n, d = x.shape
    dff = wdn.shape[0]
    row = pl.BlockSpec((tm, d), lambda i: (i, 0))
    return pl.pallas_call(
        functools.partial(_ffn_body, dff=dff),
        grid=(n // tm,),
        in_specs=[row, _const_spec((1, d)), _const_spec(wup.shape), _const_spec(wdn.shape)],
        out_specs=row,
        out_shape=jax.ShapeDtypeStruct((n, d), F32),
        scratch_shapes=[pltpu.VMEM((tm, dff), BF16)],
        compiler_params=_params(("parallel",)),
        name="ffn",
    )(x, g, wup, wdn)


def _swap_halves(x):
    lane = lax.broadcasted_iota(jnp.int32, (x.shape[0], LANES), 1)
    first = (lane % HEAD_DIM) < (HEAD_DIM // 2)
    out = []
    for c in range(x.shape[1] // LANES):
        xs = x[:, c * LANES:(c + 1) * LANES]
        up = pltpu.roll(xs, LANES - HEAD_DIM // 2, axis=1)
        down = pltpu.roll(xs, HEAD_DIM // 2, axis=1)
        out.append(jnp.where(first, up, down))
    return jnp.concatenate(out, axis=1)


def _qk_norm_rope(x, g, cos, sin_signed, group_ones):
    ms = _split_dot(x * x, group_ones) * (1.0 / HEAD_DIM)
    y = x * lax.rsqrt(ms + RMS_EPS) * g
    return y * cos + _swap_halves(y) * sin_signed


def _proj_body(x_ref, g_ref, w_ref, gq_ref, gk_ref, cos_ref, sin_ref, ones_ref,
               ka_o, va_o, kb_o, vb_o, qa_b, ka_b, va_b, qb_b, kb_b, vb_b, *, width, scale, feature_major):
    h = _rms(x_ref[...], g_ref[...]).astype(BF16)

    def seg(i):
        return _dot(h, w_ref[:, i * width:(i + 1) * width])

    def stored(x):
        return x.T if feature_major else x

    cos, sin = cos_ref[...], sin_ref[...]
    qa = _qk_norm_rope(seg(0), gq_ref[...], cos, sin, ones_ref[...])
    qa_b[...] = (qa * scale).astype(qa_b.dtype)
    ka = _qk_norm_rope(seg(1), gk_ref[...], cos, sin, ones_ref[...])
    ka_o[...] = stored(ka)
    ka_b[...] = ka.astype(BF16)
    va = seg(2)
    if feature_major:
        n_heads = width // LANES
        for hd in range(n_heads):
            va_o[pl.ds(hd, va.shape[0], stride=n_heads), :] = va[:, hd * LANES:(hd + 1) * LANES]
    else:
        va_o[...] = va
    va_b[...] = va.astype(BF16)
    qb_b[...] = (seg(3) * scale).astype(qb_b.dtype)
    kb = seg(4)
    kb_o[...] = stored(kb)
    kb_b[...] = kb.astype(BF16)
    vb = seg(5)
    vb_o[...] = stored(vb)
    vb_b[...] = vb.astype(BF16)


def _proj(x, g, w_qkv, gq, gk, cos, sin, ones, tm, q_dtype, q_scale, seq=None):
    n, d = x.shape
    width = w_qkv.shape[1] // 6
    n_pos = cos.shape[0]
    row = pl.BlockSpec((tm, d), lambda i: (i, 0))
    out = pl.BlockSpec((tm, width), lambda i: (i, 0))
    if n_pos == 1:
        pos = _const_spec((1, width))
    else:
        pos = pl.BlockSpec((tm, width), lambda i: (i % (n_pos // tm), 0))
    f32o = jax.ShapeDtypeStruct((n, width), F32)
    if seq is None:
        fm_spec, fm_o, va_spec, va_o = out, f32o, out, f32o
    else:
        per_seq = seq // tm
        fm_spec = pl.BlockSpec((None, width, tm), lambda i: (i // per_seq, 0, i % per_seq))
        fm_o = jax.ShapeDtypeStruct((n // seq, width, seq), F32)
        n_heads = width // LANES
        va_spec = pl.BlockSpec((tm * n_heads, LANES), lambda i: (i, 0))
        va_o = jax.ShapeDtypeStruct((n * n_heads, LANES), F32)
    b16o = jax.ShapeDtypeStruct((n, width), BF16)
    qo = jax.ShapeDtypeStruct((n, width), q_dtype)
    return pl.pallas_call(
        functools.partial(_proj_body, width=width, scale=q_scale, feature_major=seq is not None),
        grid=(n // tm,),
        in_specs=[row, _const_spec((1, d)), _const_spec(w_qkv.shape), _const_spec((1, width)),
                  _const_spec((1, width)), pos, pos, _const_spec(ones.shape)],
        out_specs=[fm_spec, va_spec, fm_spec, fm_spec] + [out] * 6,
        out_shape=[fm_o, va_o, fm_o, fm_o, qo, b16o, b16o, qo, b16o, b16o],
        compiler_params=_params(("parallel",)),
        name="proj",
    )(x, g, w_qkv, gq, gk, cos, sin, ones)


def _lam(lam_ref, lam_init):
    p = lam_ref[...]
    s1 = jnp.sum(p[0:1] * p[1:2], axis=-1, keepdims=True)
    s2 = jnp.sum(p[2:3] * p[3:4], axis=-1, keepdims=True)
    return jnp.exp(s1) - jnp.exp(s2) + lam_init


def _attn_a_body(q_ref, k_ref, v_ref, lam_ref, gs_ref, o_ref, *, blk, lam_init):
    i = pl.program_id(2)
    q = q_ref[...]
    lane = lax.broadcasted_iota(jnp.int32, q.shape, 1)
    qs = (jnp.where(lane < HEAD_DIM, q, 0), jnp.where(lane >= HEAD_DIM, q, 0))
    rows = lax.broadcasted_iota(jnp.int32, (blk, blk), 0)
    cols = lax.broadcasted_iota(jnp.int32, (blk, blk), 1)

    def step(j, carry, masked):
        off = pl.multiple_of(j * blk, blk)
        k = k_ref[pl.ds(off, blk), :]
        v = v_ref[pl.ds(off, blk), :]
        new = []
        for mp in range(2):
            m, l, acc = carry[3 * mp:3 * mp + 3]
            s = _dot_nt(qs[mp], k)
            if masked:
                s = jnp.where(cols <= rows, s, -jnp.inf)
            m_new = jnp.maximum(m, jnp.max(s, axis=-1, keepdims=True))
            alpha = jnp.exp(m - m_new)
            p = jnp.exp(s - m_new)
            l = alpha * l + jnp.sum(p, axis=-1, keepdims=True)
            acc = alpha * acc + _dot(p.astype(BF16), v)
            new += [m_new, l, acc]
        return tuple(new)

    init = (jnp.full((blk, 1), -jnp.inf, F32), jnp.zeros((blk, 1), F32), jnp.zeros((blk, LANES), F32)) * 2
    carry = lax.fori_loop(0, i // 2, lambda jj, cr: step(2 * jj + 1, step(2 * jj, cr, False), False), init)
    carry = lax.cond(i % 2 == 1, lambda cr: step(i - 1, cr, False), lambda cr: cr, carry)
    _, l1, a1, _, l2, a2 = step(i, carry, True)
    o = a1 / l1 - _lam(lam_ref, lam_init) * (a2 / l2)
    o_ref[...] = (_rms(o, gs_ref[...]) * (1.0 - lam_init)).astype(o_ref.dtype)


def _attn_b_body(q_ref, k_ref, v_ref, tri_ref, o_ref, *, blk):
    i = pl.program_id(2)
    q = q_ref[...]
    lane = lax.broadcasted_iota(jnp.int32, q.shape, 1)
    qs = (jnp.where(lane < HEAD_DIM, q, 0), jnp.where(lane >= HEAD_DIM, q, 0))
    rows = lax.broadcasted_iota(jnp.int32, (blk, blk), 0)
    cols = lax.broadcasted_iota(jnp.int32, (blk, blk), 1)
    tri = tri_ref[...]

    def step(j, carry, masked):
        off = pl.multiple_of(j * blk, blk)
        k = k_ref[pl.ds(off, blk), :]
        v = v_ref[pl.ds(off, blk), :]
        new = []
        for hd in range(2):
            c, acc = carry[2 * hd:2 * hd + 2]
            z = _dot_nt(qs[hd], k)
            sp = jnp.maximum(z, 0.0) + jnp.log(1.0 + jnp.exp(-jnp.abs(z)))
            if masked:
                sp = jnp.where(cols < rows, sp, 0.0)
            parts = [None] * (blk // SUFFIX_BLOCK)
            for sb in reversed(range(blk // SUFFIX_BLOCK)):
                sl = slice(sb * SUFFIX_BLOCK, (sb + 1) * SUFFIX_BLOCK)
                within = _dot(sp[:, sl].astype(BF16), tri)
                parts[sb] = jnp.exp(z[:, sl] + within + c)
                c = c - jnp.sum(sp[:, sl], axis=-1, keepdims=True)
            a = jnp.concatenate(parts, axis=1)
            if masked:
                a = jnp.where(cols < rows, a, 0.0)
            acc = acc + _dot(a.astype(BF16), v)
            new += [c, acc]
        return tuple(new)

    init = (jnp.zeros((blk, 1), F32), jnp.zeros((blk, LANES), F32)) * 2
    carry = step(i, init, True)
    carry = lax.fori_loop(0, i // 2, lambda jj, cr: step(i - 2 - 2 * jj, step(i - 1 - 2 * jj, cr, False), False),
                          carry)
    carry = lax.cond(i % 2 == 1, lambda cr: step(0, cr, False), lambda cr: cr, carry)
    o_ref[...] = jnp.where(lane < HEAD_DIM, carry[1], carry[3]).astype(o_ref.dtype)


def _prompt_attention(body, q, k, v, extra, extra_specs, batch, seq, name):
    n, width = q.shape
    blk = ATTN_BLOCK
    nq = seq // blk
    qspec = pl.BlockSpec((blk, LANES), lambda b, h, i: (b * nq + i, h))
    kvspec = pl.BlockSpec((seq, LANES), lambda b, h, i: (b, h))
    return pl.pallas_call(
        body,
        grid=(batch, width // LANES, nq),
        in_specs=[qspec, kvspec, kvspec] + extra_specs,
        out_specs=qspec,
        out_shape=jax.ShapeDtypeStruct((n, width), BF16),
        compiler_params=_params(("parallel", "parallel", "arbitrary")),
        name=name,
    )(q, k, v, *extra)


def _group_rows(x_row, n_rows):
    w = x_row.shape[1]
    r = lax.broadcasted_iota(jnp.int32, (n_rows, w), 0)
    c = lax.broadcasted_iota(jnp.int32, (n_rows, w), 1)
    return jnp.where(c // HEAD_DIM == r, jnp.broadcast_to(x_row, (n_rows, w)), 0.0)


def _decode_body(pt_ref, qa_ref, ka_ref, va_ref, qb_ref, lam_ref, gs_ref, tri_ref, *rest,
                 n_pages, page, lam_init):
    del pt_ref
    pages = rest[:4 * n_pages]
    oa_ref, ob_ref, m_scr, l_scr, acca_scr, c_scr, accb_scr = rest[4 * n_pages:]
    g = pl.program_id(1)
    rows = 8
    scale = HEAD_DIM ** -0.5
    qa_rows = _group_rows(qa_ref[...], rows)
    qa_bf = (qa_rows * scale).astype(BF16)
    qb_bf = (_group_rows(qb_ref[...], rows) * scale).astype(BF16)

    @pl.when(g == 0)
    def _():
        s_self = jnp.sum(qa_rows * ka_ref[...], axis=-1, keepdims=True) * scale
        m_scr[...] = jnp.broadcast_to(s_self, m_scr.shape)
        l_scr[...] = jnp.ones(l_scr.shape, F32)
        acca_scr[...] = jnp.broadcast_to(va_ref[...], acca_scr.shape)
        c_scr[...] = jnp.zeros(c_scr.shape, F32)
        accb_scr[...] = jnp.zeros(accb_scr.shape, F32)

    ka_pages, va_pages, kb_pages, vb_pages = (pages[t * n_pages:(t + 1) * n_pages] for t in range(4))
    n_heads = acca_scr.shape[1] // LANES

    s = jnp.concatenate([_dot(qa_bf, kp[...].astype(BF16)) for kp in ka_pages], axis=1)
    m = m_scr[...]
    m_new = jnp.maximum(m, jnp.max(s, axis=-1, keepdims=True))
    alpha = jnp.exp(m - m_new)[:, 0:1]
    pr = jnp.exp(s - m_new[:, 0:1])
    l_scr[...] = alpha * l_scr[...] + jnp.sum(pr, axis=-1, keepdims=True)
    m_scr[...] = m_new
    pr = pr.astype(BF16)
    acc_a = alpha * acca_scr[...]
    for p, vp in enumerate(va_pages):
        va = jnp.concatenate([vp[pl.ds(h, page, stride=n_heads), :] for h in range(n_heads)], axis=1)
        acc_a = acc_a + _dot(pr[:, p * page:(p + 1) * page], va.astype(BF16))
    acca_scr[...] = acc_a

    z = jnp.concatenate([_dot(qb_bf, kp[...].astype(BF16)) for kp in kb_pages], axis=1)
    log_beta = _log_sigmoid(z)
    log_1mb = log_beta - z
    hi = log_1mb.astype(BF16).astype(F32)
    hi_lo = jnp.concatenate([hi, log_1mb - hi], axis=0).astype(BF16)
    stacked = jnp.concatenate([hi_lo[:, p * page:(p + 1) * page] for p in range(n_pages)], axis=0)
    both = _dot(stacked, tri_ref[...])
    c = c_scr[...]
    weights = []
    for p in range(n_pages):
        bp = both[2 * rows * p:2 * rows * p + rows] + both[2 * rows * p + rows:2 * rows * (p + 1)]
        weights.append(jnp.exp(log_beta[:, p * page:(p + 1) * page] + bp[:, :page] + c).astype(BF16))
        c = c + bp[:, page:]
    c_scr[...] = c
    acc_b = accb_scr[...]
    for a, vp in zip(weights, vb_pages):
        acc_b = acc_b + _dot_nt(a, vp[...].astype(BF16))
    accb_scr[...] = acc_b

    @pl.when(g == pl.num_programs(1) - 1)
    def _():
        w = acc_a.shape[1]
        r = lax.broadcasted_iota(jnp.int32, (rows, w), 0)
        col = lax.broadcasted_iota(jnp.int32, (rows, w), 1)
        an = acc_a / l_scr[:, 0:1]
        own_head = col // LANES == r // 2
        o1 = jnp.sum(jnp.where(own_head & (r % 2 == 0), an, 0.0), axis=0, keepdims=True)
        o2 = jnp.sum(jnp.where(own_head & (r % 2 == 1), an, 0.0), axis=0, keepdims=True)
        o = o1 - _lam(lam_ref, lam_init) * o2
        heads = [_rms(o[:, h * LANES:(h + 1) * LANES], gs_ref[...]) for h in range(w // LANES)]
        oa_ref[...] = jnp.concatenate(heads, axis=1) * (1.0 - lam_init)
        ob_ref[...] = jnp.sum(jnp.where(col // HEAD_DIM == r, acc_b, 0.0), axis=0, keepdims=True)


def _decode(page_table, qa, ka, va, qb, lam_p, gs, tri_ones, caches, lam_init):
    nb, width = qa.shape
    n_tab = page_table.shape[1]
    page = caches[0].shape[2]
    assert all(cch.shape[1:] == (width, page) for cch in caches) and page == LANES
    npg = PAGES_PER_STEP
    steps = n_tab // npg

    def vec(x):
        return x.reshape(nb, 1, width)

    vspec = pl.BlockSpec((None, 1, width), lambda b, g, pt: (b, 0, 0))

    def page_spec(p):
        return pl.BlockSpec((None, width, page),
                            lambda b, g, pt: (pt[b * n_tab + n_tab - 1 - (g * npg + p)], 0, 0))

    page_specs = [page_spec(p) for _ in range(4) for p in range(npg)]
    page_args = [cch for cch in caches for _ in range(npg)]
    o_sds = jax.ShapeDtypeStruct((nb, 1, width), F32)
    grid_spec = pltpu.PrefetchScalarGridSpec(
        num_scalar_prefetch=1,
        grid=(nb, steps),
        in_specs=[vspec, vspec, vspec, vspec, _const_spec(lam_p.shape), _const_spec(gs.shape),
                  _const_spec(tri_ones.shape)] + page_specs,
        out_specs=[vspec, vspec],
        scratch_shapes=[pltpu.VMEM((8, LANES), F32), pltpu.VMEM((8, LANES), F32), pltpu.VMEM((8, width), F32),
                        pltpu.VMEM((8, LANES), F32), pltpu.VMEM((8, width), F32)],
    )
    oa, ob = pl.pallas_call(
        functools.partial(_decode_body, n_pages=npg, page=page, lam_init=lam_init),
        grid_spec=grid_spec,
        out_shape=[o_sds, o_sds],
        compiler_params=_params(("parallel", "arbitrary")),
        name="decode",
    )(page_table.reshape(-1), vec(qa), vec(ka), vec(va), vec(qb), lam_p, gs, tri_ones, *page_args)
    return oa.reshape(nb, width), ob.reshape(nb, width)


def _merge_body(x_ref, oa_ref, ob_ref, gm_ref, wg_ref, woa_ref, wob_ref, wout_ref,
                g2_ref, wup_ref, wdn_ref, o_ref, h_scr, *, dff):
    x = x_ref[...]
    d = x.shape[1]
    h = _rms(x, gm_ref[...]).astype(BF16)
    gate_a = 1.0 / (1.0 + jnp.exp(-_dot(h, wg_ref[:, :d])))
    gate_b = 1.0 / (1.0 + jnp.exp(-_dot(h, wg_ref[:, d:])))
    merged = (gate_a * _dot(oa_ref[...].astype(BF16), woa_ref[...])
              + gate_b * _dot(ob_ref[...].astype(BF16), wob_ref[...]))
    x = x + _dot(merged.astype(BF16), wout_ref[...])
    _swiglu_into(_rms(x, g2_ref[...]).astype(BF16), wup_ref, h_scr, dff)
    o_ref[...] = x + 0.5 * _dot(h_scr[...], wdn_ref[...])


def _merge(x, oa, ob, gm, wg, woa, wob, wout, g2, wup, wdn, tm):
    n, d = x.shape
    dff = wdn.shape[0]
    row = pl.BlockSpec((tm, d), lambda i: (i, 0))
    mix = pl.BlockSpec((tm, oa.shape[1]), lambda i: (i, 0))
    consts = [_const_spec(a.shape) for a in (gm, wg, woa, wob, wout, g2, wup, wdn)]
    return pl.pallas_call(
        functools.partial(_merge_body, dff=dff),
        grid=(n // tm,),
        in_specs=[row, mix, mix] + consts,
        out_specs=row,
        out_shape=jax.ShapeDtypeStruct((n, d), F32),
        scratch_shapes=[pltpu.VMEM((tm, dff), BF16)],
        compiler_params=_params(("parallel",)),
        name="merge",
    )(x, oa, ob, gm, wg, woa, wob, wout, g2, wup, wdn)


def _rope_tables(pos, n_groups):
    half = HEAD_DIM // 2
    inv = ROPE_THETA ** (-jnp.arange(half, dtype=F32) / half)
    ang = pos.astype(F32)[:, None] * inv[None, :]
    ang = jnp.concatenate([ang, ang], axis=-1)
    sign = jnp.concatenate([-jnp.ones((half,), F32), jnp.ones((half,), F32)])
    return jnp.tile(jnp.cos(ang), (1, n_groups)), jnp.tile(jnp.sin(ang) * sign, (1, n_groups))


def kernel(x_prompt, x_sample, cache_a_k, cache_a_v, cache_b_k, cache_b_v, page_table, g_ffn1, w_ffn1_up, w_ffn1_down, g_mix, w_in, g_q_a, g_k_a, lambda_q1, lambda_k1, lambda_q2, lambda_k2, g_subln_a, w_o_a, w_o_b, w_out, g_ffn2, w_ffn2_up, w_ffn2_down):
    batch, seq, d = x_prompt.shape
    nb, dec_seq, _ = x_sample.shape
    depth, n_phys, page = cache_a_k.shape[:3]
    assert depth == 1 and dec_seq == 1
    width = w_o_a.shape[1]
    n_groups = width // HEAD_DIM
    past_len = page_table.shape[1] * page
    lam_init = _lambda_init(0)

    def row(v):
        return v.reshape(1, -1)

    w_qkv = w_in[0, :, :6 * width].astype(BF16)
    w_gate = w_in[0, :, 6 * width:].astype(BF16)
    wup1, wdn1 = w_ffn1_up[0].astype(BF16), w_ffn1_down[0].astype(BF16)
    wup2, wdn2 = w_ffn2_up[0].astype(BF16), w_ffn2_down[0].astype(BF16)
    woa, wob, wout = w_o_a[0].astype(BF16), w_o_b[0].astype(BF16), w_out[0].astype(BF16)
    gq = jnp.tile(row(g_q_a[0]), (1, n_groups))
    gk = jnp.tile(row(g_k_a[0]), (1, n_groups))
    gs = row(g_subln_a[0])
    lam_p = jnp.stack([lambda_q1[0], lambda_k1[0], lambda_q2[0], lambda_k2[0]])
    idx = jnp.arange(width, dtype=jnp.int32) // HEAD_DIM
    group_ones = (idx[:, None] == idx[None, :]).astype(BF16)
    blk_idx = jnp.arange(SUFFIX_BLOCK, dtype=jnp.int32)
    tri = -(blk_idx[:, None] >= blk_idx[None, :]).astype(BF16)

    def trunk_front(x, pos, tm, q_dtype, q_scale, seq=None):
        x1 = _ffn(x, row(g_ffn1[0]), wup1, wdn1, tm)
        cos, sin = _rope_tables(pos, n_groups)
        return x1, _proj(x1, row(g_mix[0]), w_qkv, gq, gk, cos, sin, group_ones, tm, q_dtype, q_scale, seq)

    def trunk_back(x1, oa, ob, tm):
        return _merge(x1, oa, ob, row(g_mix[0]), w_gate, woa, wob, wout, row(g_ffn2[0]), wup2, wdn2, tm)

    n = batch * seq
    x1p, (ka_p, va_p, kb_p, vb_p, qa_h, ka_h, va_h, qb_h, kb_h, vb_h) = trunk_front(
        x_prompt.reshape(n, d), jnp.arange(seq, dtype=jnp.int32), 512, BF16, HEAD_DIM ** -0.5, seq)
    oa_p = _prompt_attention(
        functools.partial(_attn_a_body, blk=ATTN_BLOCK, lam_init=lam_init), qa_h, ka_h, va_h,
        (lam_p, gs), [_const_spec(lam_p.shape), _const_spec(gs.shape)], batch, seq, "attn_a")
    ob_p = _prompt_attention(
        functools.partial(_attn_b_body, blk=ATTN_BLOCK), qb_h, kb_h, vb_h,
        (tri,), [_const_spec(tri.shape)], batch, seq, "attn_b")
    y_prompt = trunk_back(x1p, oa_p, ob_p, 512).reshape(batch, seq, d)

    x1s, (ka_s, va_s, kb_s, vb_s, qa_s, _, _, qb_s, _, _) = trunk_front(
        x_sample.reshape(nb, d), past_len + jnp.arange(dec_seq, dtype=jnp.int32), nb, F32, 1.0)
    caches = [jnp.transpose(cache_a_k[0], (0, 2, 3, 4, 1)).reshape(n_phys, width, page),
              cache_a_v[0].reshape(n_phys, page * (width // LANES), LANES),
              jnp.transpose(cache_b_k[0], (0, 2, 3, 1)).reshape(n_phys, width, page),
              jnp.transpose(cache_b_v[0], (0, 2, 3, 1)).reshape(n_phys, width, page)]
    pg_idx = jnp.arange(page, dtype=jnp.int32)
    tri_ones = jnp.concatenate([(pg_idx[:, None] > pg_idx[None, :]).astype(BF16), jnp.ones((page, page), BF16)], axis=1)
    oa_s, ob_s = _decode(page_table, qa_s, ka_s, va_s, qb_s, lam_p, gs, tri_ones, caches, lam_init)
    y_sample = trunk_back(x1s, oa_s, ob_s, nb).reshape(nb, dec_seq, d)

    n_a, n_b = w_o_a.shape[1] // (2 * HEAD_DIM), w_o_b.shape[1] // HEAD_DIM
    return (y_prompt, y_sample,
            jnp.transpose(ka_p.reshape(1, batch, n_a, 2, HEAD_DIM, seq), (0, 1, 5, 2, 3, 4)),
            va_p.reshape(1, batch, seq, n_a, 2 * HEAD_DIM),
            jnp.transpose(kb_p.reshape(1, batch, n_b, HEAD_DIM, seq), (0, 1, 4, 2, 3)),
            jnp.transpose(vb_p.reshape(1, batch, n_b, HEAD_DIM, seq), (0, 1, 4, 2, 3)),
            ka_s.reshape(1, nb, dec_seq, n_a, 2, HEAD_DIM), va_s.reshape(1, nb, dec_seq, n_a, 2 * HEAD_DIM),
            kb_s.reshape(1, nb, dec_seq, n_b, HEAD_DIM), vb_s.reshape(1, nb, dec_seq, n_b, HEAD_DIM))
```
